```python
import math
import jax
import jax.numpy as jnp
from jax import lax
import numpy as np

D_MODEL = 1024
BATCH = 2
SEQ = 16384
DEPTH = 1

N_MEM = 256
SSD_EXPAND = 2
D_INNER = SSD_EXPAND * D_MODEL
SSD_HEAD_DIM = 64
SSD_HEADS = D_INNER // SSD_HEAD_DIM
SSD_GROUPS = 4
SSD_STATE = 128
CONV_WIDTH = 4
SSD_CHUNK = 128
XBC_DIM = D_INNER + 2 * SSD_GROUPS * SSD_STATE
MLA_HEADS = 16
QK_NOPE_DIM = 64
QK_ROPE_DIM = 32
QK_HEAD_DIM = QK_NOPE_DIM + QK_ROPE_DIM
V_HEAD_DIM = 64
Q_LORA_RANK = 384
KV_LORA_RANK = 256
ROPE_THETA = 10000.0
Q_BLOCK = 128
XA_HEADS = 4
XA_HEAD_DIM = D_MODEL // XA_HEADS
N_EXPERT_GROUPS = 4
EXPERTS_PER_GROUP = 8
N_EXPERTS = N_EXPERT_GROUPS * EXPERTS_PER_GROUP
TOP_K = 2
D_EXPERT = 256
RMS_EPS = 1e-6

IN_SPLITS = (D_INNER, XBC_DIM, SSD_HEADS, Q_LORA_RANK, KV_LORA_RANK, QK_ROPE_DIM, D_MODEL, D_MODEL)
IN_PROJ_DIM = D_INNER + XBC_DIM + SSD_HEADS + Q_LORA_RANK + KV_LORA_RANK + QK_ROPE_DIM + 2 * D_MODEL

kernel_name = 'hybrid_ssd_mla_hmoe_block'


def rms_norm(x, gain):
    xf = x.astype(jnp.float32)
    xf = xf * lax.rsqrt(jnp.mean(xf * xf, axis=-1, keepdims=True) + RMS_EPS)
    return xf.astype(x.dtype) * gain


def split_cols(x, sizes):
    offs = np.cumsum(np.array(sizes))[:-1].tolist()
    return jnp.split(x, offs, axis=-1)


def causal_depthwise_conv(x, w, b):
    k, c = w.shape
    y = lax.conv_general_dilated(x, w[:, None, :], window_strides=(1,), padding=[(k - 1, 0)],
                                 dimension_numbers=('NWC', 'WIO', 'NWC'), feature_group_count=c)
    return y + b


def apply_rope(x, positions):
    half = x.shape[-1] // 2
    inv_freq = ROPE_THETA ** (-jnp.arange(half, dtype=jnp.float32) / half)
    ang = positions.astype(jnp.float32)[..., None] * inv_freq
    cos = jnp.cos(ang)[:, :, None, :]
    sin = jnp.sin(ang)[:, :, None, :]
    xf = x.astype(jnp.float32)
    x1, x2 = xf[..., :half], xf[..., half:]
    return jnp.concatenate([x1 * cos - x2 * sin, x2 * cos + x1 * sin], axis=-1).astype(x.dtype)


def ssd_chunked_scan(xh, dt, a, bm, cm):
    b, s, h, p = xh.shape
    g, n = bm.shape[2], bm.shape[3]
    r = h // g
    nc, l = s // SSD_CHUNK, SSD_CHUNK
    x_dt = (xh.astype(jnp.float32) * dt[..., None]).reshape(b, nc, l, g, r, p)
    a_dt = (dt * a).reshape(b, nc, l, g, r).transpose(0, 3, 4, 1, 2)
    bc = bm.astype(jnp.float32).reshape(b, nc, l, g, n)
    cc = cm.astype(jnp.float32).reshape(b, nc, l, g, n)
    a_cum = jnp.cumsum(a_dt, axis=-1)
    causal = jnp.tril(jnp.ones((l, l), dtype=bool))
    seg = a_cum[..., :, None] - a_cum[..., None, :]
    decay_in = jnp.exp(jnp.where(causal, seg, -jnp.inf))
    cb = jnp.einsum('bclgn,bcsgn->bcgls', cc, bc)
    y_diag = jnp.einsum('bcgls,bgrcls,bcsgrp->bclgrp', cb, decay_in, x_dt)
    decay_to_end = jnp.exp(a_cum[..., -1:] - a_cum)
    chunk_states = jnp.einsum('bclgn,bgrcl,bclgrp->bcgrpn', bc, decay_to_end, x_dt)
    chunk_decay = jnp.exp(a_cum[..., -1])

    def step(state, inp):
        st, dec = inp
        return state * dec[..., None, None] + st, state

    init = jnp.zeros((b, g, r, p, n), jnp.float32)
    _, prev_states = lax.scan(step, init, (jnp.moveaxis(chunk_states, 1, 0), jnp.moveaxis(chunk_decay, -1, 0)))
    prev_states = jnp.moveaxis(prev_states, 0, 1)
    y_off = jnp.einsum('bclgn,bcgrpn,bgrcl->bclgrp', cc, prev_states, jnp.exp(a_cum))
    return (y_diag + y_off).reshape(b, s, h, p)


def ssd_branch(z, xbc, dt_raw, conv_w, conv_b, dt_bias, a_log, d_skip, ssd_norm):
    b, s, _ = z.shape
    xbc = jax.nn.silu(causal_depthwise_conv(xbc, conv_w, conv_b))
    xs, bm, cm = split_cols(xbc, (D_INNER, SSD_GROUPS * SSD_STATE, SSD_GROUPS * SSD_STATE))
    xh = xs.reshape(b, s, SSD_HEADS, SSD_HEAD_DIM)
    bm = bm.reshape(b, s, SSD_GROUPS, SSD_STATE)
    cm = cm.reshape(b, s, SSD_GROUPS, SSD_STATE)
    dt = jax.nn.softplus(dt_raw.astype(jnp.float32) + dt_bias.astype(jnp.float32))
    a = -jnp.exp(a_log.astype(jnp.float32))
    y = ssd_chunked_scan(xh, dt, a, bm, cm)
    y = y + d_skip.astype(jnp.float32)[:, None] * xh.astype(jnp.float32)
    y = y.reshape(b, s, D_INNER) * jax.nn.silu(z.astype(jnp.float32))
    y = y.reshape(b, s, SSD_GROUPS, D_INNER // SSD_GROUPS)
    y = y * lax.rsqrt(jnp.mean(y * y, axis=-1, keepdims=True) + RMS_EPS)
    return y.reshape(b, s, D_INNER).astype(z.dtype) * ssd_norm


def causal_block_attention(q, k, v, scale):
    b, s, h, dq = q.shape
    nb = s // Q_BLOCK
    qb = jnp.moveaxis(q.reshape(b, nb, Q_BLOCK, h, dq), 1, 0)
    kpos = jnp.arange(s)

    def one_block(args):
        i, qi = args
        sc = jnp.einsum('bqhd,bkhd->bhqk', qi, k).astype(jnp.float32) * scale
        qpos = i * Q_BLOCK + jnp.arange(Q_BLOCK)
        sc = jnp.where(kpos[None, :] <= qpos[:, None], sc, -jnp.inf)
        pr = jax.nn.softmax(sc, axis=-1).astype(v.dtype)
        return jnp.einsum('bhqk,bkhd->bqhd', pr, v)

    out = lax.map(one_block, (jnp.arange(nb), qb))
    return jnp.moveaxis(out, 0, 1).reshape(b, s, h, v.shape[-1])


def mla_branch(q_a, c_kv, k_rope, positions, q_a_norm, w_q_b, kv_a_norm, w_kv_b):
    b, s, _ = q_a.shape
    q = (rms_norm(q_a, q_a_norm) @ w_q_b).reshape(b, s, MLA_HEADS, QK_HEAD_DIM)
    kv = (rms_norm(c_kv, kv_a_norm) @ w_kv_b).reshape(b, s, MLA_HEADS, QK_NOPE_DIM + V_HEAD_DIM)
    q_nope, q_pe = q[..., :QK_NOPE_DIM], q[..., QK_NOPE_DIM:]
    k_nope, v = kv[..., :QK_NOPE_DIM], kv[..., QK_NOPE_DIM:]
    q_pe = apply_rope(q_pe, positions)
    k_pe = apply_rope(k_rope[:, :, None, :], positions)
    q = jnp.concatenate([q_nope, q_pe], axis=-1)
    k = jnp.concatenate([k_nope, jnp.broadcast_to(k_pe, (b, s, MLA_HEADS, QK_ROPE_DIM))], axis=-1)
    o = causal_block_attention(q, k, v, QK_HEAD_DIM ** -0.5)
    return o.reshape(b, s, MLA_HEADS * V_HEAD_DIM)


def memory_cross_attention(h, mem, w_xq, w_xkv, w_xo):
    b, s, _ = h.shape
    m = mem.shape[1]
    q = (h @ w_xq).reshape(b, s, XA_HEADS, XA_HEAD_DIM)
    k, v = jnp.split(mem @ w_xkv, 2, axis=-1)
    k = k.reshape(b, m, XA_HEADS, XA_HEAD_DIM)
    v = v.reshape(b, m, XA_HEADS, XA_HEAD_DIM)
    sc = jnp.einsum('bshd,bmhd->bhsm', q, k).astype(jnp.float32) * XA_HEAD_DIM ** -0.5
    pr = jax.nn.softmax(sc, axis=-1).astype(v.dtype)
    o = jnp.einsum('bhsm,bmhd->bshd', pr, v).reshape(b, s, XA_HEADS * XA_HEAD_DIM)
    return o @ w_xo


def hierarchical_moe(h, w_rg, b_rg, w_re, b_re, w_gate, w_up, w_down):
    b, s, d = h.shape
    t = h.reshape(b * s, d)
    g_logits = (t @ w_rg).astype(jnp.float32) + b_rg
    g_prob = jax.nn.softmax(g_logits, axis=-1)
    g_sel = jnp.argmax(g_logits, axis=-1)
    g_w = jnp.take_along_axis(g_prob, g_sel[:, None], axis=1)
    e_logits = ((t @ w_re).astype(jnp.float32) + b_re).reshape(-1, N_EXPERT_GROUPS, EXPERTS_PER_GROUP)
    e_in = jnp.take_along_axis(e_logits, g_sel[:, None, None], axis=1)[:, 0]
    top_v, top_i = lax.top_k(e_in, TOP_K)
    w_top = jax.nn.softmax(top_v, axis=-1) * g_w
    expert_id = g_sel[:, None] * EXPERTS_PER_GROUP + top_i
    combine = jnp.sum(jax.nn.one_hot(expert_id, N_EXPERTS, dtype=jnp.float32) * w_top[..., None], axis=1).astype(t.dtype)
    hg = jnp.einsum('td,edf->tef', t, w_gate)
    hu = jnp.einsum('td,edf->tef', t, w_up)
    act = jax.nn.silu(hg) * hu * combine[:, :, None]
    y = jnp.einsum('tef,efd->td', act, w_down)
    return y.reshape(b, s, d)


def setup_inputs(seed: int = 0) -> dict:
    key = jax.random.key(seed)
    ks = iter(jax.random.split(key, 48))
    f32 = jnp.float32
    L = DEPTH

    def nrm(shape, fan_in):
        return jax.random.normal(next(ks), shape, f32) * fan_in ** -0.5

    def gain(shape):
        return 1.0 + 0.02 * jax.random.normal(next(ks), shape, f32)

    x = jax.random.normal(next(ks), (BATCH, SEQ, D_MODEL), f32)
    mem = jax.random.normal(next(ks), (BATCH, N_MEM, D_MODEL), f32)
    offset = jax.random.randint(next(ks), (BATCH, 1), 0, 1024, dtype=jnp.int32)
    positions = (offset + jnp.arange(SEQ, dtype=jnp.int32)[None, :]).astype(jnp.int32)
    dt0 = jnp.exp(jax.random.uniform(next(ks), (L, SSD_HEADS), f32, math.log(1e-3), math.log(1e-1)))
    dt_bias = dt0 + jnp.log(-jnp.expm1(-dt0))
    a_log = jnp.log(jax.random.uniform(next(ks), (L, SSD_HEADS), f32, 1.0, 16.0))
    return {
        'x': x,
        'mem': mem,
        'positions': positions,
        'norm_mix': gain((L, D_MODEL)),
        'w_in': nrm((L, D_MODEL, IN_PROJ_DIM), D_MODEL),
        'conv_w': nrm((L, CONV_WIDTH, XBC_DIM), CONV_WIDTH),
        'conv_b': 0.02 * jax.random.normal(next(ks), (L, XBC_DIM), f32),
        'dt_bias': dt_bias,
        'a_log': a_log,
        'd_skip': 1.0 + 0.1 * jax.random.normal(next(ks), (L, SSD_HEADS), f32),
        'ssd_norm': gain((L, D_INNER)),
        'w_ssd_out': nrm((L, D_INNER, D_MODEL), D_INNER),
        'q_a_norm': gain((L, Q_LORA_RANK)),
        'w_q_b': nrm((L, Q_LORA_RANK, MLA_HEADS * QK_HEAD_DIM), Q_LORA_RANK),
        'kv_a_norm': gain((L, KV_LORA_RANK)),
        'w_kv_b': nrm((L, KV_LORA_RANK, MLA_HEADS * (QK_NOPE_DIM + V_HEAD_DIM)), KV_LORA_RANK),
        'w_mla_out': nrm((L, MLA_HEADS * V_HEAD_DIM, D_MODEL), MLA_HEADS * V_HEAD_DIM),
        'w_o': nrm((L, D_MODEL, D_MODEL), D_MODEL),
        'norm_xattn': gain((L, D_MODEL)),
        'norm_mem': gain((L, D_MODEL)),
        'w_xq': nrm((L, D_MODEL, XA_HEADS * XA_HEAD_DIM), D_MODEL),
        'w_xkv': nrm((L, D_MODEL, 2 * XA_HEADS * XA_HEAD_DIM), D_MODEL),
        'w_xo': nrm((L, XA_HEADS * XA_HEAD_DIM, D_MODEL), XA_HEADS * XA_HEAD_DIM),
        'norm_moe': gain((L, D_MODEL)),
        'w_router_group': nrm((L, D_MODEL, N_EXPERT_GROUPS), D_MODEL),
        'b_router_group': 0.01 * jax.random.normal(next(ks), (L, N_EXPERT_GROUPS), f32),
        'w_router_expert': nrm((L, D_MODEL, N_EXPERTS), D_MODEL),
        'b_router_expert': 0.01 * jax.random.normal(next(ks), (L, N_EXPERTS), f32),
        'w_exp_gate': nrm((L, N_EXPERTS, D_MODEL, D_EXPERT), D_MODEL),
        'w_exp_up': nrm((L, N_EXPERTS, D_MODEL, D_EXPERT), D_MODEL),
        'w_exp_down': nrm((L, N_EXPERTS, D_EXPERT, D_MODEL), D_EXPERT),
        'norm_final': gain((D_MODEL,)),
    }


def reference(x, mem, positions, norm_mix, w_in, conv_w, conv_b, dt_bias, a_log, d_skip, ssd_norm, w_ssd_out,
              q_a_norm, w_q_b, kv_a_norm, w_kv_b, w_mla_out, w_o, norm_xattn, norm_mem, w_xq, w_xkv, w_xo,
              norm_moe, w_router_group, b_router_group, w_router_expert, b_router_expert,
              w_exp_gate, w_exp_up, w_exp_down, norm_final):
    h = x
    for i in range(DEPTH):
        u = rms_norm(h, norm_mix[i])
        z, xbc, dt_raw, q_a, c_kv, k_rope, gate_ssd, gate_mla = split_cols(u @ w_in[i], IN_SPLITS)
        y_ssd = ssd_branch(z, xbc, dt_raw, conv_w[i], conv_b[i], dt_bias[i], a_log[i], d_skip[i], ssd_norm[i]) @ w_ssd_out[i]
        y_mla = mla_branch(q_a, c_kv, k_rope, positions, q_a_norm[i], w_q_b[i], kv_a_norm[i], w_kv_b[i]) @ w_mla_out[i]
        merged = jax.nn.sigmoid(gate_ssd) * y_ssd + jax.nn.sigmoid(gate_mla) * y_mla
        h = h + merged @ w_o[i]
        h = h + memory_cross_attention(rms_norm(h, norm_xattn[i]), rms_norm(mem, norm_mem[i]), w_xq[i], w_xkv[i], w_xo[i])
        h = h + hierarchical_moe(rms_norm(h, norm_moe[i]), w_router_group[i], b_router_group[i],
                                 w_router_expert[i], b_router_expert[i], w_exp_gate[i], w_exp_up[i], w_exp_down[i])
    return rms_norm(h, norm_final)
```

```python
import functools

import jax
import jax.numpy as jnp
from jax import lax
from jax.experimental import pallas as pl
from jax.experimental.pallas import tpu as pltpu

F32 = jnp.float32
BF16 = jnp.bfloat16
RMS_EPS = 1e-6
NEG = -1e30
ROPE_THETA = 10000.0

LANES = 128
SSD_CHUNK = 128
SSD_HEAD_DIM = 64
SSD_GROUPS = 4
SSD_STATE = 128
CONV_WIDTH = 4
MLA_HEADS = 16
QK_NOPE = 64
QK_ROPE = 32
V_HEAD = 64
XA_HEADS = 4
EXPERTS_PER_GROUP = 8
VMEM_LIMIT = 56 * 1024 * 1024


def _params(n_axes):
    return pltpu.CompilerParams(dimension_semantics=("arbitrary",) * n_axes,
                                vmem_limit_bytes=VMEM_LIMIT)


def _sigmoid(v):
    return 1.0 / (1.0 + jnp.exp(-v))


def _silu(v):
    return v * _sigmoid(v)


def _split_bf16(v):
    hi = v.astype(BF16)
    lo = (v - hi.astype(F32)).astype(BF16)
    return hi, lo


def _const_spec(shape):
    return pl.BlockSpec(shape, lambda *_: (0,) * len(shape))


def _norm_matmul_kernel(x_ref, g_ref, w_ref, o_ref, un_ref):
    @pl.when(pl.program_id(1) == 0)
    def _():
        x = x_ref[...]
        r = lax.rsqrt(jnp.mean(x * x, axis=-1, keepdims=True) + RMS_EPS)
        un_ref[...] = (x * r * g_ref[...]).astype(BF16)

    o_ref[...] = jnp.dot(un_ref[...], w_ref[...], preferred_element_type=F32).astype(o_ref.dtype)


def _norm_matmul(x, gain, w, out_dtype, tm, tn):
    m, k = x.shape
    n = w.shape[1]
    tm, tn = min(tm, m), min(tn, n)
    return pl.pallas_call(
        _norm_matmul_kernel,
        grid=(m // tm, n // tn),
        in_specs=[pl.BlockSpec((tm, k), lambda i, j: (i, 0)),
                  pl.BlockSpec((1, k), lambda i, j: (0, 0)),
                  pl.BlockSpec((k, tn), lambda i, j: (0, j))],
        out_specs=pl.BlockSpec((tm, tn), lambda i, j: (i, j)),
        out_shape=jax.ShapeDtypeStruct((m, n), out_dtype),
        scratch_shapes=[pltpu.VMEM((tm, k), BF16)],
        compiler_params=_params(2),
        name="norm_matmul",
    )(x, gain.reshape(1, k), w)


def _ssd_kernel(xbc_ref, z_ref, dtr_ref, cw_ref, cb_ref, dtb_ref, alog_ref, expand_ref, dskip_ref, gn_ref,
                o_ref, cbuf_ref, state_ref, *, d_inner, heads):
    l = SSD_CHUNK
    gn = SSD_GROUPS * SSD_STATE
    gw = d_inner // SSD_GROUPS
    heads_per_group = heads // SSD_GROUPS

    @pl.when(pl.program_id(1) == 0)
    def _():
        cbuf_ref[0:8, :] = jnp.zeros((8, cbuf_ref.shape[1]), F32)
        state_ref[...] = jnp.zeros(state_ref.shape, F32)

    cbuf_ref[8:8 + l, :] = xbc_ref[0].astype(F32)
    conv = cb_ref[...] + cw_ref[3:4, :] * cbuf_ref[8:8 + l, :]
    for k in range(CONV_WIDTH - 1):
        shift = CONV_WIDTH - 1 - k
        conv = conv + cw_ref[k:k + 1, :] * cbuf_ref[8 - shift:8 - shift + l, :]
    cbuf_ref[0:8, :] = cbuf_ref[l:l + 8, :]
    conv = _silu(conv)
    xs = conv[:, :d_inner]
    bm = conv[:, d_inner:d_inner + gn]
    cm = conv[:, d_inner + gn:]

    dtv = dtr_ref[0] + dtb_ref[...]
    dt = jnp.maximum(dtv, 0.0) + jnp.log1p(jnp.exp(-jnp.abs(dtv)))
    a_dt = dt * (-jnp.exp(alog_ref[...]))
    row_i = lax.broadcasted_iota(jnp.int32, (l, l), 0)
    col_i = lax.broadcasted_iota(jnp.int32, (l, l), 1)
    causal = col_i <= row_i
    tri = causal.astype(BF16)
    hi, lo = _split_bf16(a_dt)
    a_cum = jnp.dot(tri, hi, preferred_element_type=F32) + jnp.dot(tri, lo, preferred_element_type=F32)
    a_cum_t = a_cum.T

    ex = expand_ref[...]

    def expand(v):
        vh, vl = _split_bf16(v)
        return jnp.dot(vh, ex, preferred_element_type=F32) + jnp.dot(vl, ex, preferred_element_type=F32)

    dt_x = expand(dt)
    a_cum_x = expand(a_cum)
    last = a_cum_x[l - 1:l, :]
    grow = jnp.exp(a_cum_x)
    to_end = jnp.exp(last - a_cum_x)
    chunk_decay = jnp.exp(last)

    x_dt = xs * dt_x
    x_dt_b = x_dt.astype(BF16)
    x_end_b = (x_dt * to_end).astype(BF16)

    lane = lax.broadcasted_iota(jnp.int32, (l, LANES), 1)
    first_head = lane < SSD_HEAD_DIM
    outs = []
    for g in range(SSD_GROUPS):
        bg = bm[:, g * SSD_STATE:(g + 1) * SSD_STATE]
        cg = cm[:, g * SSD_STATE:(g + 1) * SSD_STATE].astype(BF16)
        cbm = lax.dot_general(cg, bg.astype(BF16), (((1,), (1,)), ((), ())), preferred_element_type=F32)
        cols = slice(g * gw, (g + 1) * gw)
        diag = []
        for pair in range(heads_per_group // 2):
            h0 = g * heads_per_group + 2 * pair
            xp = x_dt_b[:, h0 * SSD_HEAD_DIM:(h0 + 2) * SSD_HEAD_DIM]
            ys = []
            for h in (h0, h0 + 1):
                seg = a_cum[:, h:h + 1] - a_cum_t[h:h + 1, :]
                mh = (cbm * jnp.exp(jnp.where(causal, seg, NEG))).astype(BF16)
                ys.append(jnp.dot(mh, xp, preferred_element_type=F32))
            diag.append(jnp.where(first_head, ys[0], ys[1]))
        st = state_ref[g]
        y_off = jnp.dot(cg, st.astype(BF16), preferred_element_type=F32) * grow[:, cols]
        state_ref[g] = st * chunk_decay[:, cols] + jnp.dot(bg.T.astype(BF16), x_end_b[:, cols],
                                                          preferred_element_type=F32)
        outs.append(jnp.concatenate(diag, axis=1) + y_off)
    y = jnp.concatenate(outs, axis=1) + dskip_ref[...] * xs

    zz = z_ref[0].astype(F32)
    y = y * _silu(zz)
    normed = []
    for g in range(SSD_GROUPS):
        yg = y[:, g * gw:(g + 1) * gw]
        normed.append(yg * lax.rsqrt(jnp.mean(yg * yg, axis=-1, keepdims=True) + RMS_EPS))
    o_ref[0] = (jnp.concatenate(normed, axis=1) * gn_ref[...]).astype(o_ref.dtype)


def _ssd(xbc, z, dtr, conv_w, conv_b, dt_bias, a_log, d_skip, ssd_norm):
    b, s, xdim = xbc.shape
    d_inner = z.shape[2]
    heads = a_log.shape[0]
    l = SSD_CHUNK
    pad = LANES - heads
    dtb = jnp.pad(dt_bias.astype(F32), (0, pad)).reshape(1, LANES)
    alog = jnp.pad(a_log.astype(F32), (0, pad)).reshape(1, LANES)
    expand = (jnp.arange(LANES)[:, None] == (jnp.arange(d_inner)[None, :] // SSD_HEAD_DIM)).astype(BF16)
    dskip = jnp.repeat(d_skip.astype(F32), SSD_HEAD_DIM).reshape(1, d_inner)
    kern = functools.partial(_ssd_kernel, d_inner=d_inner, heads=heads)
    return pl.pallas_call(
        kern,
        grid=(b, s // l),
        in_specs=[pl.BlockSpec((1, l, xdim), lambda i, c: (i, c, 0)),
                  pl.BlockSpec((1, l, d_inner), lambda i, c: (i, c, 0)),
                  pl.BlockSpec((1, l, LANES), lambda i, c: (i, c, 0)),
                  _const_spec((CONV_WIDTH, xdim)),
                  _const_spec((1, xdim)),
                  _const_spec((1, LANES)),
                  _const_spec((1, LANES)),
                  _const_spec((LANES, d_inner)),
                  _const_spec((1, d_inner)),
                  _const_spec((1, d_inner))],
        out_specs=pl.BlockSpec((1, l, d_inner), lambda i, c: (i, c, 0)),
        out_shape=jax.ShapeDtypeStruct((b, s, d_inner), BF16),
        scratch_shapes=[pltpu.VMEM((l + 8, xdim), F32),
                        pltpu.VMEM((SSD_GROUPS, SSD_STATE, d_inner // SSD_GROUPS), F32)],
        compiler_params=_params(2),
        name="ssd_scan",
    )(xbc, z, dtr, conv_w.astype(F32), conv_b.astype(F32).reshape(1, xdim), dtb, alog, expand, dskip,
      ssd_norm.astype(F32).reshape(1, d_inner))


def _mla_prep_kernel(a_ref, pos_ref, invf_ref, qg_ref, kg_ref, wq1_ref, wq2_ref, wk1_ref, wk2_ref, wv_ref,
                     q_ref, k_ref, v_ref, *, q_rank, kv_rank, scale):
    a = a_ref[...]
    qa = a[:, :q_rank]
    ckv = a[:, q_rank:q_rank + kv_rank]
    rest = a[:, q_rank + kv_rank:]
    qn = (qa * lax.rsqrt(jnp.mean(qa * qa, axis=-1, keepdims=True) + RMS_EPS) * qg_ref[...]).astype(BF16)
    cn = (ckv * lax.rsqrt(jnp.mean(ckv * ckv, axis=-1, keepdims=True) + RMS_EPS) * kg_ref[...]).astype(BF16)
    rest_b = rest.astype(BF16)

    ang = pos_ref[...].astype(F32) * invf_ref[...]
    cos = jnp.concatenate([jnp.cos(ang)] * MLA_HEADS, axis=1)
    sin = jnp.concatenate([jnp.sin(ang)] * MLA_HEADS, axis=1)

    q = jnp.dot(qn, wq1_ref[...], preferred_element_type=F32) * cos
    q = q + jnp.dot(qn, wq2_ref[...], preferred_element_type=F32) * sin
    q_ref[...] = (q * scale).astype(q_ref.dtype)

    kin = jnp.concatenate([cn, rest_b], axis=1)
    k = jnp.dot(kin, wk1_ref[...], preferred_element_type=F32) * cos
    k = k + jnp.dot(rest_b, wk2_ref[...], preferred_element_type=F32) * sin
    k_ref[...] = k.astype(k_ref.dtype)

    v = jnp.dot(cn, wv_ref[...], preferred_element_type=F32)
    lane = lax.broadcasted_iota(jnp.int32, v.shape, 1) % LANES
    v_ref[...] = jnp.where(lane == V_HEAD, 1.0, v).astype(v_ref.dtype)


def _mla_prep(qkv_a, positions, q_a_norm, w_q_b, kv_a_norm, w_kv_b, tm):
    t = qkv_a.shape[0]
    q_rank, kv_rank = w_q_b.shape[0], w_kv_b.shape[0]
    hd = MLA_HEADS * LANES
    qk_head = QK_NOPE + QK_ROPE
    half = QK_ROPE // 2
    tm = min(tm, t)

    w3 = w_q_b.astype(F32).reshape(q_rank, MLA_HEADS, qk_head)
    wq1 = jnp.pad(w3, ((0, 0), (0, 0), (0, LANES - qk_head))).reshape(q_rank, hd).astype(BF16)
    pe = w3[..., QK_NOPE:]
    rot = jnp.concatenate([-pe[..., half:], pe[..., :half]], axis=-1)
    wq2 = jnp.pad(rot, ((0, 0), (0, 0), (QK_NOPE, LANES - qk_head))).reshape(q_rank, hd).astype(BF16)

    kv3 = w_kv_b.astype(F32).reshape(kv_rank, MLA_HEADS, QK_NOPE + V_HEAD)
    wk_nope = jnp.pad(kv3[..., :QK_NOPE], ((0, 0), (0, 0), (0, LANES - QK_NOPE))).reshape(kv_rank, hd)
    wv = jnp.pad(kv3[..., QK_NOPE:], ((0, 0), (0, 0), (0, LANES - V_HEAD))).reshape(kv_rank, hd).astype(BF16)
    eye = jnp.eye(QK_ROPE, dtype=F32)
    place = jnp.pad(eye, ((0, LANES - QK_ROPE), (QK_NOPE, LANES - qk_head)))
    eye_rot = jnp.concatenate([-eye[:, half:], eye[:, :half]], axis=1)
    place_rot = jnp.pad(eye_rot, ((0, LANES - QK_ROPE), (QK_NOPE, LANES - qk_head)))
    wk1 = jnp.concatenate([wk_nope, jnp.tile(place, (1, MLA_HEADS))], axis=0).astype(BF16)
    wk2 = jnp.tile(place_rot, (1, MLA_HEADS)).astype(BF16)

    inv_freq = ROPE_THETA ** (-jnp.arange(half, dtype=F32) / half)
    invf = jnp.concatenate([jnp.zeros((QK_NOPE,), F32), inv_freq, inv_freq,
                            jnp.zeros((LANES - qk_head,), F32)]).reshape(1, LANES)

    width = qkv_a.shape[1]
    kern = functools.partial(_mla_prep_kernel, q_rank=q_rank, kv_rank=kv_rank, scale=qk_head ** -0.5)
    out = jax.ShapeDtypeStruct((t, hd), BF16)
    return pl.pallas_call(
        kern,
        grid=(t // tm,),
        in_specs=[pl.BlockSpec((tm, width), lambda i: (i, 0)),
                  pl.BlockSpec((tm, 1), lambda i: (i, 0)),
                  _const_spec((1, LANES)),
                  _const_spec((1, q_rank)),
                  _const_spec((1, kv_rank)),
                  _const_spec((q_rank, hd)),
                  _const_spec((q_rank, hd)),
                  _const_spec((kv_rank + LANES, hd)),
                  _const_spec((LANES, hd)),
                  _const_spec((kv_rank, hd))],
        out_specs=[pl.BlockSpec((tm, hd), lambda i: (i, 0))] * 3,
        out_shape=[out, out, out],
        compiler_params=_params(1),
        name="mla_prep",
    )(qkv_a, positions.reshape(t, 1), invf, q_a_norm.astype(F32).reshape(1, q_rank),
      kv_a_norm.astype(F32).reshape(1, kv_rank), wq1, wq2, wk1, wk2, wv)


def _flash_kernel(q_ref, k_ref, v_ref, o_ref, m_ref, acc_ref, *, tq):
    i = pl.program_id(2)
    q = q_ref[0]
    m_ref[...] = jnp.full(m_ref.shape, NEG, F32)
    acc_ref[...] = jnp.zeros(acc_ref.shape, F32)

    def step(j, masked):
        kb = k_ref[0, pl.ds(pl.multiple_of(j * tq, tq), tq), :]
        vb = v_ref[0, pl.ds(pl.multiple_of(j * tq, tq), tq), :]
        s = lax.dot_general(q, kb, (((1,), (1,)), ((), ())), preferred_element_type=F32)
        if masked:
            r = lax.broadcasted_iota(jnp.int32, s.shape, 0)
            c = lax.broadcasted_iota(jnp.int32, s.shape, 1)
            s = jnp.where(c <= r, s, NEG)
        m_old = m_ref[...]
        m_new = jnp.maximum(m_old, jnp.max(s, axis=-1, keepdims=True))
        p = jnp.exp(s - m_new).astype(BF16)
        acc_ref[...] = jnp.exp(m_old - m_new) * acc_ref[...] + jnp.dot(p, vb, preferred_element_type=F32)
        m_ref[...] = m_new

    def body(j, carry):
        step(j, False)
        return carry

    lax.fori_loop(0, i, body, 0)
    step(i, True)
    acc = acc_ref[...]
    o_ref[0] = (acc / acc[:, V_HEAD:V_HEAD + 1]).astype(o_ref.dtype)


def _flash(q, k, v, tq):
    b, s, hd = q.shape
    heads = hd // LANES
    tq = min(tq, s)
    return pl.pallas_call(
        functools.partial(_flash_kernel, tq=tq),
        grid=(b, heads, s // tq),
        in_specs=[pl.BlockSpec((1, tq, LANES), lambda bi, h, i: (bi, i, h)),
                  pl.BlockSpec((1, s, LANES), lambda bi, h, i: (bi, 0, h)),
                  pl.BlockSpec((1, s, LANES), lambda bi, h, i: (bi, 0, h))],
        out_specs=pl.BlockSpec((1, tq, LANES), lambda bi, h, i: (bi, i, h)),
        out_shape=jax.ShapeDtypeStruct((b, s, hd), BF16),
        scratch_shapes=[pltpu.VMEM((tq, 1), F32), pltpu.VMEM((tq, LANES), F32)],
        compiler_params=_params(3),
        name="mla_flash",
    )(q, k, v)


def _merge_kernel(yn_ref, o_ref, g_ref, x_ref, wso_ref, wmo_ref, wo_ref, h_ref, *, d_model):
    y_ssd = jnp.dot(yn_ref[...], wso_ref[...], preferred_element_type=F32)
    y_mla = jnp.dot(o_ref[...], wmo_ref[...], preferred_element_type=F32)
    g = g_ref[...].astype(F32)
    merged = _sigmoid(g[:, :d_model]) * y_ssd + _sigmoid(g[:, d_model:]) * y_mla
    h_ref[...] = x_ref[...] + jnp.dot(merged.astype(BF16), wo_ref[...], preferred_element_type=F32)


def _merge(yn, o, gates, x, w_ssd_out, w_mla_pad, w_o, tm):
    t, d_model = x.shape
    tm = min(tm, t)
    row = lambda width: pl.BlockSpec((tm, width), lambda i: (i, 0))
    return pl.pallas_call(
        functools.partial(_merge_kernel, d_model=d_model),
        grid=(t // tm,),
        in_specs=[row(yn.shape[1]), row(o.shape[1]), row(gates.shape[1]), row(d_model),
                  _const_spec(w_ssd_out.shape), _const_spec(w_mla_pad.shape), _const_spec(w_o.shape)],
        out_specs=row(d_model),
        out_shape=jax.ShapeDtypeStruct((t, d_model), F32),
        compiler_params=_params(1),
        name="merge",
    )(yn, o, gates, x, w_ssd_out, w_mla_pad, w_o)


def _xattn_router_kernel(h_ref, kv_ref, gx_ref, gm_ref, wq_ref, wo_ref, wrh_ref, wrl_ref, br_ref,
                         h2_ref, hm_ref, comb_ref, *, d_model):
    h1 = h_ref[...]
    hn = (h1 * lax.rsqrt(jnp.mean(h1 * h1, axis=-1, keepdims=True) + RMS_EPS) * gx_ref[...]).astype(BF16)
    q = jnp.dot(hn, wq_ref[...], preferred_element_type=F32).astype(BF16)
    hd = d_model // XA_HEADS
    kv = kv_ref[0]
    outs = []
    for hh in range(XA_HEADS):
        qh = q[:, hh * hd:(hh + 1) * hd]
        kh = kv[:, hh * hd:(hh + 1) * hd]
        vh = kv[:, d_model + hh * hd:d_model + (hh + 1) * hd]
        s = lax.dot_general(qh, kh, (((1,), (1,)), ((), ())), preferred_element_type=F32) * (hd ** -0.5)
        p = jnp.exp(s - jnp.max(s, axis=-1, keepdims=True))
        p = p / jnp.sum(p, axis=-1, keepdims=True)
        outs.append(jnp.dot(p.astype(BF16), vh, preferred_element_type=F32))
    ox = jnp.concatenate(outs, axis=1).astype(BF16)
    h2 = h1 + jnp.dot(ox, wo_ref[...], preferred_element_type=F32)
    h2_ref[...] = h2

    hm = h2 * lax.rsqrt(jnp.mean(h2 * h2, axis=-1, keepdims=True) + RMS_EPS) * gm_ref[...]
    hm_ref[...] = hm.astype(hm_ref.dtype)

    hi, lo = _split_bf16(hm)
    logits = (jnp.dot(hi, wrh_ref[...], preferred_element_type=F32)
              + jnp.dot(lo, wrh_ref[...], preferred_element_type=F32)
              + jnp.dot(hi, wrl_ref[...], preferred_element_type=F32)) + br_ref[...]
    gl = logits[:, :LANES]
    el = logits[:, LANES:]
    lane = lax.broadcasted_iota(jnp.int32, gl.shape, 1)

    def first_argmax(v, vmax):
        return jnp.min(jnp.where(v == vmax, lane, LANES), axis=-1, keepdims=True)

    gmax = jnp.max(gl, axis=-1, keepdims=True)
    g_sel = first_argmax(gl, gmax)
    g_w = 1.0 / jnp.sum(jnp.exp(gl - gmax), axis=-1, keepdims=True)
    lo_lane = g_sel * EXPERTS_PER_GROUP
    in_group = (lane >= lo_lane) & (lane < lo_lane + EXPERTS_PER_GROUP)
    e1 = jnp.where(in_group, el, NEG)
    v1 = jnp.max(e1, axis=-1, keepdims=True)
    i1 = first_argmax(e1, v1)
    e2 = jnp.where(lane == i1, NEG, e1)
    v2 = jnp.max(e2, axis=-1, keepdims=True)
    i2 = first_argmax(e2, v2)
    r = jnp.exp(v2 - v1)
    w1 = g_w / (1.0 + r)
    w2 = g_w * r / (1.0 + r)
    comb_ref[...] = jnp.where(lane == i1, w1, 0.0) + jnp.where(lane == i2, w2, 0.0)


def _xattn_router(h1, kvx, tokens_per_batch, norm_xattn, norm_moe, w_xq, w_xo, wr_hi, wr_lo, b_r, tm):
    t, d_model = h1.shape
    tm = min(tm, tokens_per_batch)
    per_b = tokens_per_batch // tm
    n_mem = kvx.shape[1]
    row = lambda width: pl.BlockSpec((tm, width), lambda i: (i, 0))
    return pl.pallas_call(
        functools.partial(_xattn_router_kernel, d_model=d_model),
        grid=(t // tm,),
        in_specs=[row(d_model),
                  pl.BlockSpec((1, n_mem, 2 * d_model), lambda i: (i // per_b, 0, 0)),
                  _const_spec((1, d_model)), _const_spec((1, d_model)),
                  _const_spec(w_xq.shape), _const_spec(w_xo.shape),
                  _const_spec(wr_hi.shape), _const_spec(wr_lo.shape), _const_spec(b_r.shape)],
        out_specs=[row(d_model), row(d_model), row(LANES)],
        out_shape=[jax.ShapeDtypeStruct((t, d_model), F32),
                   jax.ShapeDtypeStruct((t, d_model), BF16),
                   jax.ShapeDtypeStruct((t, LANES), F32)],
        compiler_params=_params(1),
        name="xattn_router",
    )(h1, kvx, norm_xattn.astype(F32).reshape(1, d_model), norm_moe.astype(F32).reshape(1, d_model),
      w_xq, w_xo, wr_hi, wr_lo, b_r)


def _moe_kernel(hm_ref, comb_ref, h2_ref, wg_ref, wu_ref, wd_ref, gf_ref, o_ref, acc_ref, *, final_norm):
    e = pl.program_id(1)

    @pl.when(e == 0)
    def _():
        acc_ref[...] = jnp.zeros(acc_ref.shape, F32)

    t = hm_ref[...]
    comb = comb_ref[...]
    lane = lax.broadcasted_iota(jnp.int32, comb.shape, 1)
    w = jnp.sum(jnp.where(lane == e, comb, 0.0), axis=-1, keepdims=True)
    hg = jnp.dot(t, wg_ref[0], preferred_element_type=F32)
    hu = jnp.dot(t, wu_ref[0], preferred_element_type=F32)
    act = (_silu(hg) * hu * w).astype(BF16)
    acc_ref[...] += jnp.dot(act, wd_ref[0], preferred_element_type=F32)

    @pl.when(e == pl.num_programs(1) - 1)
    def _():
        h3 = h2_ref[...] + acc_ref[...]
        if final_norm:
            h3 = h3 * lax.rsqrt(jnp.mean(h3 * h3, axis=-1, keepdims=True) + RMS_EPS) * gf_ref[...]
        o_ref[...] = h3


def _moe(hm, comb, h2, w_gate, w_up, w_down, norm_final, final_norm, tm):
    t, d_model = hm.shape
    n_exp, _, d_exp = w_gate.shape
    tm = min(tm, t)
    row = lambda width: pl.BlockSpec((tm, width), lambda i, e: (i, 0))
    return pl.pallas_call(
        functools.partial(_moe_kernel, final_norm=final_norm),
        grid=(t // tm, n_exp),
        in_specs=[row(d_model), row(LANES), row(d_model),
                  pl.BlockSpec((1, d_model, d_exp), lambda i, e: (e, 0, 0)),
                  pl.BlockSpec((1, d_model, d_exp), lambda i, e: (e, 0, 0)),
                  pl.BlockSpec((1, d_exp, d_model), lambda i, e: (e, 0, 0)),
                  pl.BlockSpec((1, d_model), lambda i, e: (0, 0))],
        out_specs=row(d_model),
        out_shape=jax.ShapeDtypeStruct((t, d_model), F32),
        scratch_shapes=[pltpu.VMEM((tm, d_model), F32)],
        compiler_params=_params(2),
        name="moe_experts",
    )(hm, comb, h2, w_gate, w_up, w_down, norm_final.astype(F32).reshape(1, d_model))


def kernel(x, mem, positions, norm_mix, w_in, conv_w, conv_b, dt_bias, a_log, d_skip, ssd_norm, w_ssd_out, q_a_norm, w_q_b, kv_a_norm, w_kv_b, w_mla_out, w_o, norm_xattn, norm_mem, w_xq, w_xkv, w_xo, norm_moe, w_router_group, b_router_group, w_router_expert, b_router_expert, w_exp_gate, w_exp_up, w_exp_down, norm_final):
    b, s, d_model = x.shape
    t = b * s
    n_mem = mem.shape[1]
    depth = norm_mix.shape[0]
    d_inner = ssd_norm.shape[1]
    xbc_dim = conv_w.shape[2]
    heads = a_log.shape[1]
    q_rank, kv_rank = w_q_b.shape[1], w_kv_b.shape[1]
    n_groups = w_router_group.shape[2]
    n_exp = w_router_expert.shape[2]

    o_z, o_xbc = 0, d_inner
    o_dt = o_xbc + xbc_dim
    o_qa = o_dt + heads
    o_gs = o_qa + q_rank + kv_rank + QK_ROPE
    o_end = o_gs + 2 * d_model

    h = x.reshape(t, d_model).astype(F32)
    for i in range(depth):
        wi = w_in[i]
        w_z = wi[:, o_z:o_xbc].astype(BF16)
        w_xbc = wi[:, o_xbc:o_dt].astype(BF16)
        w_dt = jnp.pad(wi[:, o_dt:o_qa], ((0, 0), (0, LANES - heads))).astype(BF16)
        w_qkv = jnp.pad(wi[:, o_qa:o_gs], ((0, 0), (0, LANES - QK_ROPE))).astype(BF16)
        w_gates = wi[:, o_gs:o_end].astype(BF16)

        zb = _norm_matmul(h, norm_mix[i], w_z, BF16, 1024, 1024).reshape(b, s, d_inner)
        xbc = _norm_matmul(h, norm_mix[i], w_xbc, BF16, 1024, 1024).reshape(b, s, xbc_dim)
        gates = _norm_matmul(h, norm_mix[i], w_gates, BF16, 1024, 1024)
        dtr = _norm_matmul(h, norm_mix[i], w_dt, F32, 1024, LANES).reshape(b, s, LANES)
        qkv_a = _norm_matmul(h, norm_mix[i], w_qkv, F32, 1024, w_qkv.shape[1])

        yn = _ssd(xbc, zb, dtr, conv_w[i], conv_b[i], dt_bias[i], a_log[i], d_skip[i], ssd_norm[i])

        hd = MLA_HEADS * LANES
        q, k, v = _mla_prep(qkv_a, positions, q_a_norm[i], w_q_b[i], kv_a_norm[i], w_kv_b[i], 512)
        o = _flash(q.reshape(b, s, hd), k.reshape(b, s, hd), v.reshape(b, s, hd), 512)

        w_mla_pad = jnp.pad(w_mla_out[i].reshape(MLA_HEADS, V_HEAD, d_model),
                            ((0, 0), (0, LANES - V_HEAD), (0, 0))).reshape(hd, d_model).astype(BF16)
        h1 = _merge(yn.reshape(t, d_inner), o.reshape(t, hd), gates, h, w_ssd_out[i].astype(BF16), w_mla_pad,
                    w_o[i].astype(BF16), 256)

        kvx = _norm_matmul(mem.reshape(b * n_mem, d_model).astype(F32), norm_mem[i], w_xkv[i].astype(BF16), BF16,
                           b * n_mem, 1024).reshape(b, n_mem, 2 * d_model)
        w_r = jnp.concatenate([jnp.pad(w_router_group[i].astype(F32), ((0, 0), (0, LANES - n_groups))),
                               jnp.pad(w_router_expert[i].astype(F32), ((0, 0), (0, LANES - n_exp)))], axis=1)
        wr_hi = w_r.astype(BF16)
        wr_lo = (w_r - wr_hi.astype(F32)).astype(BF16)
        b_r = jnp.concatenate([jnp.pad(b_router_group[i].astype(F32), (0, LANES - n_groups), constant_values=NEG),
                               jnp.pad(b_router_expert[i].astype(F32), (0, LANES - n_exp), constant_values=NEG)]
                              ).reshape(1, 2 * LANES)
        h2, hm, comb = _xattn_router(h1, kvx, s, norm_xattn[i], norm_moe[i], w_xq[i].astype(BF16),
                                     w_xo[i].astype(BF16), wr_hi, wr_lo, b_r, 512)

        h = _moe(hm, comb, h2, w_exp_gate[i].astype(BF16), w_exp_up[i].astype(BF16), w_exp_down[i].astype(BF16),
                 norm_final, i == depth - 1, 1024)
    return h.reshape(b, s, d_model)
```

```python
import functools

import jax
import jax.numpy as jnp
from jax import lax
from jax.experimental import pallas as pl
from jax.experimental.pallas import tpu as pltpu

F32 = jnp.float32
BF16 = jnp.bfloat16
RMS_EPS = 1e-6
NEG = -1e30
ROPE_THETA = 10000.0
LOG2_E = 1.4426950408889634

LANES = 128
SSD_CHUNK = 128
SSD_HEAD_DIM = 64
SSD_GROUPS = 4
SSD_STATE = 128
CONV_WIDTH = 4
MLA_HEADS = 16
QK_NOPE = 64
QK_ROPE = 32
V_HEAD = 64
XA_HEADS = 4
EXPERTS_PER_GROUP = 8
VMEM_LIMIT = 56 * 1024 * 1024


def _params(n_axes):
    return pltpu.CompilerParams(dimension_semantics=("arbitrary",) * n_axes,
                                vmem_limit_bytes=VMEM_LIMIT)


def _sigmoid(v):
    return 1.0 / (1.0 + jnp.exp(-v))


def _silu(v):
    return v * _sigmoid(v)


def _split_bf16(v):
    hi = v.astype(BF16)
    lo = (v - hi.astype(F32)).astype(BF16)
    return hi, lo


def _const_spec(shape):
    return pl.BlockSpec(shape, lambda *_: (0,) * len(shape))


def _norm_matmul_kernel(x_ref, g_ref, w_ref, o_ref, un_ref):
    @pl.when(pl.program_id(1) == 0)
    def _():
        x = x_ref[...]
        r = lax.rsqrt(jnp.mean(x * x, axis=-1, keepdims=True) + RMS_EPS)
        un_ref[...] = (x * r * g_ref[...]).astype(BF16)

    o_ref[...] = jnp.dot(un_ref[...], w_ref[...], preferred_element_type=F32).astype(o_ref.dtype)


def _norm_matmul(x, gain, w, out_dtype, tm, tn):
    m, k = x.shape
    n = w.shape[1]
    tm, tn = min(tm, m), min(tn, n)
    return pl.pallas_call(
        _norm_matmul_kernel,
        grid=(m // tm, n // tn),
        in_specs=[pl.BlockSpec((tm, k), lambda i, j: (i, 0)),
                  pl.BlockSpec((1, k), lambda i, j: (0, 0)),
                  pl.BlockSpec((k, tn), lambda i, j: (0, j))],
        out_specs=pl.BlockSpec((tm, tn), lambda i, j: (i, j)),
        out_shape=jax.ShapeDtypeStruct((m, n), out_dtype),
        scratch_shapes=[pltpu.VMEM((tm, k), BF16)],
        compiler_params=_params(2),
        name="norm_matmul",
    )(x, gain.reshape(1, k), w)


def _ssd_kernel(xbc_ref, z_ref, dtr_ref, cw_ref, cb_ref, dtb_ref, alog_ref, expand_ref, dskip_ref, gn_ref,
                o_ref, cbuf_ref, state_ref, *, d_inner, heads):
    l = SSD_CHUNK
    gn = SSD_GROUPS * SSD_STATE
    gw = d_inner // SSD_GROUPS
    heads_per_group = heads // SSD_GROUPS

    @pl.when(pl.program_id(1) == 0)
    def _():
        cbuf_ref[0:8, :] = jnp.zeros((8, cbuf_ref.shape[1]), F32)
        state_ref[...] = jnp.zeros(state_ref.shape, F32)

    cbuf_ref[8:8 + l, :] = xbc_ref[0].astype(F32)
    conv = cb_ref[...] + cw_ref[3:4, :] * cbuf_ref[8:8 + l, :]
    for k in range(CONV_WIDTH - 1):
        shift = CONV_WIDTH - 1 - k
        conv = conv + cw_ref[k:k + 1, :] * cbuf_ref[8 - shift:8 - shift + l, :]
    cbuf_ref[0:8, :] = cbuf_ref[l:l + 8, :]
    conv = _silu(conv)
    xs = conv[:, :d_inner]
    bm = conv[:, d_inner:d_inner + gn]
    cm = conv[:, d_inner + gn:]

    dtv = dtr_ref[0] + dtb_ref[...]
    dt = jnp.maximum(dtv, 0.0) + jnp.log1p(jnp.exp(-jnp.abs(dtv)))
    a_dt = dt * (-jnp.exp(alog_ref[...]))
    row_i = lax.broadcasted_iota(jnp.int32, (l, l), 0)
    col_i = lax.broadcasted_iota(jnp.int32, (l, l), 1)
    causal = col_i <= row_i
    tri = causal.astype(BF16)
    hi, lo = _split_bf16(a_dt)
    a_cum = jnp.dot(tri, hi, preferred_element_type=F32) + jnp.dot(tri, lo, preferred_element_type=F32)
    a_cum_t = a_cum.T

    ex = expand_ref[...]

    def expand(v):
        vh, vl = _split_bf16(v)
        return jnp.dot(vh, ex, preferred_element_type=F32) + jnp.dot(vl, ex, preferred_element_type=F32)

    dt_x = expand(dt)
    a_cum_x = expand(a_cum)
    last = a_cum_x[l - 1:l, :]
    grow = jnp.exp(a_cum_x)
    to_end = jnp.exp(last - a_cum_x)
    chunk_decay = jnp.exp(last)

    x_dt = xs * dt_x
    x_dt_b = x_dt.astype(BF16)
    x_end_b = (x_dt * to_end).astype(BF16)

    lane = lax.broadcasted_iota(jnp.int32, (l, LANES), 1)
    first_head = lane < SSD_HEAD_DIM
    outs = []
    for g in range(SSD_GROUPS):
        bg = bm[:, g * SSD_STATE:(g + 1) * SSD_STATE]
        cg = cm[:, g * SSD_STATE:(g + 1) * SSD_STATE].astype(BF16)
        cbm = lax.dot_general(cg, bg.astype(BF16), (((1,), (1,)), ((), ())), preferred_element_type=F32)
        cols = slice(g * gw, (g + 1) * gw)
        diag = []
        for pair in range(heads_per_group // 2):
            h0 = g * heads_per_group + 2 * pair
            xp = x_dt_b[:, h0 * SSD_HEAD_DIM:(h0 + 2) * SSD_HEAD_DIM]
            ys = []
            for h in (h0, h0 + 1):
                seg = a_cum[:, h:h + 1] - a_cum_t[h:h + 1, :]
                mh = (cbm * jnp.exp(jnp.where(causal, seg, NEG))).astype(BF16)
                ys.append(jnp.dot(mh, xp, preferred_element_type=F32))
            diag.append(jnp.where(first_head, ys[0], ys[1]))
        st = state_ref[g]
        y_off = jnp.dot(cg, st.astype(BF16), preferred_element_type=F32) * grow[:, cols]
        state_ref[g] = st * chunk_decay[:, cols] + jnp.dot(bg.T.astype(BF16), x_end_b[:, cols],
                                                          preferred_element_type=F32)
        outs.append(jnp.concatenate(diag, axis=1) + y_off)
    y = jnp.concatenate(outs, axis=1) + dskip_ref[...] * xs

    zz = z_ref[0].astype(F32)
    y = y * _silu(zz)
    normed = []
    for g in range(SSD_GROUPS):
        yg = y[:, g * gw:(g + 1) * gw]
        normed.append(yg * lax.rsqrt(jnp.mean(yg * yg, axis=-1, keepdims=True) + RMS_EPS))
    o_ref[0] = (jnp.concatenate(normed, axis=1) * gn_ref[...]).astype(o_ref.dtype)


def _ssd(xbc, z, dtr, conv_w, conv_b, dt_bias, a_log, d_skip, ssd_norm):
    b, s, xdim = xbc.shape
    d_inner = z.shape[2]
    heads = a_log.shape[0]
    l = SSD_CHUNK
    pad = LANES - heads
    dtb = jnp.pad(dt_bias.astype(F32), (0, pad)).reshape(1, LANES)
    alog = jnp.pad(a_log.astype(F32), (0, pad)).reshape(1, LANES)
    expand = (jnp.arange(LANES)[:, None] == (jnp.arange(d_inner)[None, :] // SSD_HEAD_DIM)).astype(BF16)
    dskip = jnp.repeat(d_skip.astype(F32), SSD_HEAD_DIM).reshape(1, d_inner)
    kern = functools.partial(_ssd_kernel, d_inner=d_inner, heads=heads)
    return pl.pallas_call(
        kern,
        grid=(b, s // l),
        in_specs=[pl.BlockSpec((1, l, xdim), lambda i, c: (i, c, 0)),
                  pl.BlockSpec((1, l, d_inner), lambda i, c: (i, c, 0)),
                  pl.BlockSpec((1, l, LANES), lambda i, c: (i, c, 0)),
                  _const_spec((CONV_WIDTH, xdim)),
                  _const_spec((1, xdim)),
                  _const_spec((1, LANES)),
                  _const_spec((1, LANES)),
                  _const_spec((LANES, d_inner)),
                  _const_spec((1, d_inner)),
                  _const_spec((1, d_inner))],
        out_specs=pl.BlockSpec((1, l, d_inner), lambda i, c: (i, c, 0)),
        out_shape=jax.ShapeDtypeStruct((b, s, d_inner), BF16),
        scratch_shapes=[pltpu.VMEM((l + 8, xdim), F32),
                        pltpu.VMEM((SSD_GROUPS, SSD_STATE, d_inner // SSD_GROUPS), F32)],
        compiler_params=_params(2),
        name="ssd_scan",
    )(xbc, z, dtr, conv_w.astype(F32), conv_b.astype(F32).reshape(1, xdim), dtb, alog, expand, dskip,
      ssd_norm.astype(F32).reshape(1, d_inner))


def _mla_prep_kernel(a_ref, pos_ref, invf_ref, qg_ref, kg_ref, wq1_ref, wq2_ref, wk1_ref, wk2_ref, wv_ref,
                     q_ref, k_ref, v_ref, *, q_rank, kv_rank, scale):
    a = a_ref[...]
    qa = a[:, :q_rank]
    ckv = a[:, q_rank:q_rank + kv_rank]
    rest = a[:, q_rank + kv_rank:]
    qn = (qa * lax.rsqrt(jnp.mean(qa * qa, axis=-1, keepdims=True) + RMS_EPS) * qg_ref[...]).astype(BF16)
    cn = (ckv * lax.rsqrt(jnp.mean(ckv * ckv, axis=-1, keepdims=True) + RMS_EPS) * kg_ref[...]).astype(BF16)
    rest_b = rest.astype(BF16)

    ang = pos_ref[...].astype(F32) * invf_ref[...]
    cos = jnp.concatenate([jnp.cos(ang)] * MLA_HEADS, axis=1)
    sin = jnp.concatenate([jnp.sin(ang)] * MLA_HEADS, axis=1)

    q = jnp.dot(qn, wq1_ref[...], preferred_element_type=F32) * cos
    q = q + jnp.dot(qn, wq2_ref[...], preferred_element_type=F32) * sin
    q_ref[...] = (q * scale).astype(q_ref.dtype)

    kin = jnp.concatenate([cn, rest_b], axis=1)
    k = jnp.dot(kin, wk1_ref[...], preferred_element_type=F32) * cos
    k = k + jnp.dot(rest_b, wk2_ref[...], preferred_element_type=F32) * sin
    k_ref[...] = k.astype(k_ref.dtype)

    v = jnp.dot(cn, wv_ref[...], preferred_element_type=F32)
    lane = lax.broadcasted_iota(jnp.int32, v.shape, 1) % LANES
    v_ref[...] = jnp.where(lane == V_HEAD, 1.0, v).astype(v_ref.dtype)


def _mla_prep(qkv_a, positions, q_a_norm, w_q_b, kv_a_norm, w_kv_b, tm):
    t = qkv_a.shape[0]
    q_rank, kv_rank = w_q_b.shape[0], w_kv_b.shape[0]
    hd = MLA_HEADS * LANES
    qk_head = QK_NOPE + QK_ROPE
    half = QK_ROPE // 2
    tm = min(tm, t)

    w3 = w_q_b.astype(F32).reshape(q_rank, MLA_HEADS, qk_head)
    wq1 = jnp.pad(w3, ((0, 0), (0, 0), (0, LANES - qk_head))).reshape(q_rank, hd).astype(BF16)
    pe = w3[..., QK_NOPE:]
    rot = jnp.concatenate([-pe[..., half:], pe[..., :half]], axis=-1)
    wq2 = jnp.pad(rot, ((0, 0), (0, 0), (QK_NOPE, LANES - qk_head))).reshape(q_rank, hd).astype(BF16)

    kv3 = w_kv_b.astype(F32).reshape(kv_rank, MLA_HEADS, QK_NOPE + V_HEAD)
    wk_nope = jnp.pad(kv3[..., :QK_NOPE], ((0, 0), (0, 0), (0, LANES - QK_NOPE))).reshape(kv_rank, hd)
    wv = jnp.pad(kv3[..., QK_NOPE:], ((0, 0), (0, 0), (0, LANES - V_HEAD))).reshape(kv_rank, hd).astype(BF16)
    eye = jnp.eye(QK_ROPE, dtype=F32)
    place = jnp.pad(eye, ((0, LANES - QK_ROPE), (QK_NOPE, LANES - qk_head)))
    eye_rot = jnp.concatenate([-eye[:, half:], eye[:, :half]], axis=1)
    place_rot = jnp.pad(eye_rot, ((0, LANES - QK_ROPE), (QK_NOPE, LANES - qk_head)))
    wk1 = jnp.concatenate([wk_nope, jnp.tile(place, (1, MLA_HEADS))], axis=0).astype(BF16)
    wk2 = jnp.tile(place_rot, (1, MLA_HEADS)).astype(BF16)

    inv_freq = ROPE_THETA ** (-jnp.arange(half, dtype=F32) / half)
    invf = jnp.concatenate([jnp.zeros((QK_NOPE,), F32), inv_freq, inv_freq,
                            jnp.zeros((LANES - qk_head,), F32)]).reshape(1, LANES)

    width = qkv_a.shape[1]
    kern = functools.partial(_mla_prep_kernel, q_rank=q_rank, kv_rank=kv_rank,
                             scale=qk_head ** -0.5 * LOG2_E)
    out = jax.ShapeDtypeStruct((t, hd), BF16)
    return pl.pallas_call(
        kern,
        grid=(t // tm,),
        in_specs=[pl.BlockSpec((tm, width), lambda i: (i, 0)),
                  pl.BlockSpec((tm, 1), lambda i: (i, 0)),
                  _const_spec((1, LANES)),
                  _const_spec((1, q_rank)),
                  _const_spec((1, kv_rank)),
                  _const_spec((q_rank, hd)),
                  _const_spec((q_rank, hd)),
                  _const_spec((kv_rank + LANES, hd)),
                  _const_spec((LANES, hd)),
                  _const_spec((kv_rank, hd))],
        out_specs=[pl.BlockSpec((tm, hd), lambda i: (i, 0))] * 3,
        out_shape=[out, out, out],
        compiler_params=_params(1),
        name="mla_prep",
    )(qkv_a, positions.reshape(t, 1), invf, q_a_norm.astype(F32).reshape(1, q_rank),
      kv_a_norm.astype(F32).reshape(1, kv_rank), wq1, wq2, wk1, wk2, wv)


def _flash_kernel(q_ref, k_ref, v_ref, o_ref, sa_ref, sb_ref, m_ref, acc_ref, *, tq):
    i = pl.program_id(2)
    q = q_ref[0]
    m_ref[...] = jnp.full(m_ref.shape, NEG, F32)
    acc_ref[...] = jnp.zeros(acc_ref.shape, F32)
    n_tiles = tq // LANES

    def rows(j):
        return pl.ds(pl.multiple_of(j * tq, tq), tq)

    def scores(j, s_ref):
        s_ref[...] = lax.dot_general(q, k_ref[0, rows(j), :], (((1,), (1,)), ((), ())),
                                     preferred_element_type=F32)

    def update(j, s_ref, masked):
        tiles = [s_ref[:, c * LANES:(c + 1) * LANES] for c in range(n_tiles)]
        if masked:
            r = lax.broadcasted_iota(jnp.int32, (tq, LANES), 0)
            c0 = lax.broadcasted_iota(jnp.int32, (tq, LANES), 1)
            tiles = [jnp.where(c0 + c * LANES <= r, t, NEG) for c, t in enumerate(tiles)]
        m_old = m_ref[...]
        m_new = jnp.maximum(m_old, jnp.max(functools.reduce(jnp.maximum, tiles), axis=-1, keepdims=True))
        p = jnp.concatenate([jnp.exp2(t - m_new) for t in tiles], axis=1).astype(BF16)
        acc_ref[...] = (jnp.exp2(m_old - m_new) * acc_ref[...]
                        + jnp.dot(p, v_ref[0, rows(j), :], preferred_element_type=F32))
        m_ref[...] = m_new

    scores(0, sa_ref)

    def body(jj, carry):
        j = 2 * jj
        scores(j + 1, sb_ref)
        update(j, sa_ref, False)
        scores(j + 2, sa_ref)
        update(j + 1, sb_ref, False)
        return carry

    lax.fori_loop(0, i // 2, body, 0)

    @pl.when(i % 2 == 0)
    def _():
        update(i, sa_ref, True)

    @pl.when(i % 2 == 1)
    def _():
        scores(i, sb_ref)
        update(i - 1, sa_ref, False)
        update(i, sb_ref, True)

    acc = acc_ref[...]
    o_ref[0] = (acc / acc[:, V_HEAD:V_HEAD + 1]).astype(o_ref.dtype)


def _flash(q, k, v, tq):
    b, s, hd = q.shape
    heads = hd // LANES
    tq = min(tq, s)
    return pl.pallas_call(
        functools.partial(_flash_kernel, tq=tq),
        grid=(b, heads, s // tq),
        in_specs=[pl.BlockSpec((1, tq, LANES), lambda bi, h, i: (bi, i, h)),
                  pl.BlockSpec((1, s, LANES), lambda bi, h, i: (bi, 0, h)),
                  pl.BlockSpec((1, s, LANES), lambda bi, h, i: (bi, 0, h))],
        out_specs=pl.BlockSpec((1, tq, LANES), lambda bi, h, i: (bi, i, h)),
        out_shape=jax.ShapeDtypeStruct((b, s, hd), BF16),
        scratch_shapes=[pltpu.VMEM((tq, tq), F32), pltpu.VMEM((tq, tq), F32),
                        pltpu.VMEM((tq, LANES), F32), pltpu.VMEM((tq, LANES), F32)],
        compiler_params=_params(3),
        name="mla_flash",
    )(q, k, v)


def _merge_kernel(yn_ref, o_ref, g_ref, x_ref, wso_ref, wmo_ref, wo_ref, h_ref, *, d_model):
    y_ssd = jnp.dot(yn_ref[...], wso_ref[...], preferred_element_type=F32)
    y_mla = jnp.dot(o_ref[...], wmo_ref[...], preferred_element_type=F32)
    g = g_ref[...].astype(F32)
    merged = _sigmoid(g[:, :d_model]) * y_ssd + _sigmoid(g[:, d_model:]) * y_mla
    h_ref[...] = x_ref[...] + jnp.dot(merged.astype(BF16), wo_ref[...], preferred_element_type=F32)


def _merge(yn, o, gates, x, w_ssd_out, w_mla_pad, w_o, tm):
    t, d_model = x.shape
    tm = min(tm, t)
    row = lambda width: pl.BlockSpec((tm, width), lambda i: (i, 0))
    return pl.pallas_call(
        functools.partial(_merge_kernel, d_model=d_model),
        grid=(t // tm,),
        in_specs=[row(yn.shape[1]), row(o.shape[1]), row(gates.shape[1]), row(d_model),
                  _const_spec(w_ssd_out.shape), _const_spec(w_mla_pad.shape), _const_spec(w_o.shape)],
        out_specs=row(d_model),
        out_shape=jax.ShapeDtypeStruct((t, d_model), F32),
        compiler_params=_params(1),
        name="merge",
    )(yn, o, gates, x, w_ssd_out, w_mla_pad, w_o)


def _xattn_router_kernel(h_ref, kv_ref, gx_ref, gm_ref, wq_ref, wo_ref, wrh_ref, wrl_ref, br_ref,
                         h2_ref, hm_ref, comb_ref, *, d_model):
    h1 = h_ref[...]
    hn = (h1 * lax.rsqrt(jnp.mean(h1 * h1, axis=-1, keepdims=True) + RMS_EPS) * gx_ref[...]).astype(BF16)
    q = jnp.dot(hn, wq_ref[...], preferred_element_type=F32).astype(BF16)
    hd = d_model // XA_HEADS
    kv = kv_ref[0]
    outs = []
    for hh in range(XA_HEADS):
        qh = q[:, hh * hd:(hh + 1) * hd]
        kh = kv[:, hh * hd:(hh + 1) * hd]
        vh = kv[:, d_model + hh * hd:d_model + (hh + 1) * hd]
        s = lax.dot_general(qh, kh, (((1,), (1,)), ((), ())), preferred_element_type=F32) * (hd ** -0.5)
        p = jnp.exp(s - jnp.max(s, axis=-1, keepdims=True))
        p = p / jnp.sum(p, axis=-1, keepdims=True)
        outs.append(jnp.dot(p.astype(BF16), vh, preferred_element_type=F32))
    ox = jnp.concatenate(outs, axis=1).astype(BF16)
    h2 = h1 + jnp.dot(ox, wo_ref[...], preferred_element_type=F32)
    h2_ref[...] = h2

    hm = h2 * lax.rsqrt(jnp.mean(h2 * h2, axis=-1, keepdims=True) + RMS_EPS) * gm_ref[...]
    hm_ref[...] = hm.astype(hm_ref.dtype)

    hi, lo = _split_bf16(hm)
    logits = (jnp.dot(hi, wrh_ref[...], preferred_element_type=F32)
              + jnp.dot(lo, wrh_ref[...], preferred_element_type=F32)
              + jnp.dot(hi, wrl_ref[...], preferred_element_type=F32)) + br_ref[...]
    gl = logits[:, :LANES]
    el = logits[:, LANES:]
    lane = lax.broadcasted_iota(jnp.int32, gl.shape, 1)

    def first_argmax(v, vmax):
        return jnp.min(jnp.where(v == vmax, lane, LANES), axis=-1, keepdims=True)

    gmax = jnp.max(gl, axis=-1, keepdims=True)
    g_sel = first_argmax(gl, gmax)
    g_w = 1.0 / jnp.sum(jnp.exp(gl - gmax), axis=-1, keepdims=True)
    lo_lane = g_sel * EXPERTS_PER_GROUP
    in_group = (lane >= lo_lane) & (lane < lo_lane + EXPERTS_PER_GROUP)
    e1 = jnp.where(in_group, el, NEG)
    v1 = jnp.max(e1, axis=-1, keepdims=True)
    i1 = first_argmax(e1, v1)
    e2 = jnp.where(lane == i1, NEG, e1)
    v2 = jnp.max(e2, axis=-1, keepdims=True)
    i2 = first_argmax(e2, v2)
    r = jnp.exp(v2 - v1)
    w1 = g_w / (1.0 + r)
    w2 = g_w * r / (1.0 + r)
    comb_ref[...] = jnp.where(lane == i1, w1, 0.0) + jnp.where(lane == i2, w2, 0.0)


def _xattn_router(h1, kvx, tokens_per_batch, norm_xattn, norm_moe, w_xq, w_xo, wr_hi, wr_lo, b_r, tm):
    t, d_model = h1.shape
    tm = min(tm, tokens_per_batch)
    per_b = tokens_per_batch // tm
    n_mem = kvx.shape[1]
    row = lambda width: pl.BlockSpec((tm, width), lambda i: (i, 0))
    return pl.pallas_call(
        functools.partial(_xattn_router_kernel, d_model=d_model),
        grid=(t // tm,),
        in_specs=[row(d_model),
                  pl.BlockSpec((1, n_mem, 2 * d_model), lambda i: (i // per_b, 0, 0)),
                  _const_spec((1, d_model)), _const_spec((1, d_model)),
                  _const_spec(w_xq.shape), _const_spec(w_xo.shape),
                  _const_spec(wr_hi.shape), _const_spec(wr_lo.shape), _const_spec(b_r.shape)],
        out_specs=[row(d_model), row(d_model), row(LANES)],
        out_shape=[jax.ShapeDtypeStruct((t, d_model), F32),
                   jax.ShapeDtypeStruct((t, d_model), BF16),
                   jax.ShapeDtypeStruct((t, LANES), F32)],
        compiler_params=_params(1),
        name="xattn_router",
    )(h1, kvx, norm_xattn.astype(F32).reshape(1, d_model), norm_moe.astype(F32).reshape(1, d_model),
      w_xq, w_xo, wr_hi, wr_lo, b_r)


def _moe_kernel(hm_ref, comb_ref, h2_ref, wg_ref, wu_ref, wd_ref, gf_ref, o_ref, acc_ref, *, final_norm):
    e = pl.program_id(1)

    @pl.when(e == 0)
    def _():
        acc_ref[...] = jnp.zeros(acc_ref.shape, F32)

    t = hm_ref[...]
    comb = comb_ref[...]
    lane = lax.broadcasted_iota(jnp.int32, comb.shape, 1)
    w = jnp.sum(jnp.where(lane == e, comb, 0.0), axis=-1, keepdims=True)
    hg = jnp.dot(t, wg_ref[0], preferred_element_type=F32)
    hu = jnp.dot(t, wu_ref[0], preferred_element_type=F32)
    act = (_silu(hg) * hu * w).astype(BF16)
    acc_ref[...] += jnp.dot(act, wd_ref[0], preferred_element_type=F32)

    @pl.when(e == pl.num_programs(1) - 1)
    def _():
        h3 = h2_ref[...] + acc_ref[...]
        if final_norm:
            h3 = h3 * lax.rsqrt(jnp.mean(h3 * h3, axis=-1, keepdims=True) + RMS_EPS) * gf_ref[...]
        o_ref[...] = h3


def _moe(hm, comb, h2, w_gate, w_up, w_down, norm_final, final_norm, tm):
    t, d_model = hm.shape
    n_exp, _, d_exp = w_gate.shape
    tm = min(tm, t)
    row = lambda width: pl.BlockSpec((tm, width), lambda i, e: (i, 0))
    return pl.pallas_call(
        functools.partial(_moe_kernel, final_norm=final_norm),
        grid=(t // tm, n_exp),
        in_specs=[row(d_model), row(LANES), row(d_model),
                  pl.BlockSpec((1, d_model, d_exp), lambda i, e: (e, 0, 0)),
                  pl.BlockSpec((1, d_model, d_exp), lambda i, e: (e, 0, 0)),
                  pl.BlockSpec((1, d_exp, d_model), lambda i, e: (e, 0, 0)),
                  pl.BlockSpec((1, d_model), lambda i, e: (0, 0))],
        out_specs=row(d_model),
        out_shape=jax.ShapeDtypeStruct((t, d_model), F32),
        scratch_shapes=[pltpu.VMEM((tm, d_model), F32)],
        compiler_params=_params(2),
        name="moe_experts",
    )(hm, comb, h2, w_gate, w_up, w_down, norm_final.astype(F32).reshape(1, d_model))


def kernel(x, mem, positions, norm_mix, w_in, conv_w, conv_b, dt_bias, a_log, d_skip, ssd_norm, w_ssd_out, q_a_norm, w_q_b, kv_a_norm, w_kv_b, w_mla_out, w_o, norm_xattn, norm_mem, w_xq, w_xkv, w_xo, norm_moe, w_router_group, b_router_group, w_router_expert, b_router_expert, w_exp_gate, w_exp_up, w_exp_down, norm_final):
    b, s, d_model = x.shape
    t = b * s
    n_mem = mem.shape[1]
    depth = norm_mix.shape[0]
    d_inner = ssd_norm.shape[1]
    xbc_dim = conv_w.shape[2]
    heads = a_log.shape[1]
    q_rank, kv_rank = w_q_b.shape[1], w_kv_b.shape[1]
    n_groups = w_router_group.shape[2]
    n_exp = w_router_expert.shape[2]

    o_z, o_xbc = 0, d_inner
    o_dt = o_xbc + xbc_dim
    o_qa = o_dt + heads
    o_gs = o_qa + q_rank + kv_rank + QK_ROPE
    o_end = o_gs + 2 * d_model

    h = x.reshape(t, d_model).astype(F32)
    for i in range(depth):
        wi = w_in[i]
        w_z = wi[:, o_z:o_xbc].astype(BF16)
        w_xbc = wi[:, o_xbc:o_dt].astype(BF16)
        w_dt = jnp.pad(wi[:, o_dt:o_qa], ((0, 0), (0, LANES - heads))).astype(BF16)
        w_qkv = jnp.pad(wi[:, o_qa:o_gs], ((0, 0), (0, LANES - QK_ROPE))).astype(BF16)
        w_gates = wi[:, o_gs:o_end].astype(BF16)

        zb = _norm_matmul(h, norm_mix[i], w_z, BF16, 1024, 1024).reshape(b, s, d_inner)
        xbc = _norm_matmul(h, norm_mix[i], w_xbc, BF16, 1024, 1024).reshape(b, s, xbc_dim)
        gates = _norm_matmul(h, norm_mix[i], w_gates, BF16, 1024, 1024)
        dtr = _norm_matmul(h, norm_mix[i], w_dt, F32, 1024, LANES).reshape(b, s, LANES)
        qkv_a = _norm_matmul(h, norm_mix[i], w_qkv, F32, 1024, w_qkv.shape[1])

        yn = _ssd(xbc, zb, dtr, conv_w[i], conv_b[i], dt_bias[i], a_log[i], d_skip[i], ssd_norm[i])

        hd = MLA_HEADS * LANES
        q, k, v = _mla_prep(qkv_a, positions, q_a_norm[i], w_q_b[i], kv_a_norm[i], w_kv_b[i], 512)
        o = _flash(q.reshape(b, s, hd), k.reshape(b, s, hd), v.reshape(b, s, hd), 512)

        w_mla_pad = jnp.pad(w_mla_out[i].reshape(MLA_HEADS, V_HEAD, d_model),
                            ((0, 0), (0, LANES - V_HEAD), (0, 0))).reshape(hd, d_model).astype(BF16)
        h1 = _merge(yn.reshape(t, d_inner), o.reshape(t, hd), gates, h, w_ssd_out[i].astype(BF16), w_mla_pad,
                    w_o[i].astype(BF16), 256)

        kvx = _norm_matmul(mem.reshape(b * n_mem, d_model).astype(F32), norm_mem[i], w_xkv[i].astype(BF16), BF16,
                           b * n_mem, 1024).reshape(b, n_mem, 2 * d_model)
        w_r = jnp.concatenate([jnp.pad(w_router_group[i].astype(F32), ((0, 0), (0, LANES - n_groups))),
                               jnp.pad(w_router_expert[i].astype(F32), ((0, 0), (0, LANES - n_exp)))], axis=1)
        wr_hi = w_r.astype(BF16)
        wr_lo = (w_r - wr_hi.astype(F32)).astype(BF16)
        b_r = jnp.concatenate([jnp.pad(b_router_group[i].astype(F32), (0, LANES - n_groups), constant_values=NEG),
                               jnp.pad(b_router_expert[i].astype(F32), (0, LANES - n_exp), constant_values=NEG)]
                              ).reshape(1, 2 * LANES)
        h2, hm, comb = _xattn_router(h1, kvx, s, norm_xattn[i], norm_moe[i], w_xq[i].astype(BF16),
                                     w_xo[i].astype(BF16), wr_hi, wr_lo, b_r, 512)

        h = _moe(hm, comb, h2, w_exp_gate[i].astype(BF16), w_exp_up[i].astype(BF16), w_exp_down[i].astype(BF16),
                 norm_final, i == depth - 1, 1024)
    return h.reshape(b, s, d_model)
```

```python
import functools

import jax
import jax.numpy as jnp
from jax import lax
from jax.experimental import pallas as pl
from jax.experimental.pallas import tpu as pltpu

F32 = jnp.float32
BF16 = jnp.bfloat16
RMS_EPS = 1e-6
NEG = -1e30
ROPE_THETA = 10000.0
LOG2_E = 1.4426950408889634

LANES = 128
SSD_CHUNK = 128
SSD_HEAD_DIM = 64
SSD_GROUPS = 4
SSD_STATE = 128
CONV_WIDTH = 4
MLA_HEADS = 16
QK_NOPE = 64
QK_ROPE = 32
V_HEAD = 64
XA_HEADS = 4
EXPERTS_PER_GROUP = 8
VMEM_LIMIT = 56 * 1024 * 1024
FLASH_TQ = 1024
FLASH_TK = 512
FLASH_HP = 2


def _params(n_axes):
    return pltpu.CompilerParams(dimension_semantics=("arbitrary",) * n_axes,
                                vmem_limit_bytes=VMEM_LIMIT)


def _sigmoid(v):
    return 1.0 / (1.0 + jnp.exp(-v))


def _silu(v):
    return v * _sigmoid(v)


def _split_bf16(v):
    hi = v.astype(BF16)
    lo = (v - hi.astype(F32)).astype(BF16)
    return hi, lo


def _const_spec(shape):
    return pl.BlockSpec(shape, lambda *_: (0,) * len(shape))


def _norm_matmul_kernel(x_ref, g_ref, w_ref, o_ref, un_ref):
    @pl.when(pl.program_id(1) == 0)
    def _():
        x = x_ref[...]
        r = lax.rsqrt(jnp.mean(x * x, axis=-1, keepdims=True) + RMS_EPS)
        un_ref[...] = (x * r * g_ref[...]).astype(BF16)

    o_ref[...] = jnp.dot(un_ref[...], w_ref[...], preferred_element_type=F32).astype(o_ref.dtype)


def _norm_matmul(x, gain, w, out_dtype, tm, tn):
    m, k = x.shape
    n = w.shape[1]
    tm, tn = min(tm, m), min(tn, n)
    return pl.pallas_call(
        _norm_matmul_kernel,
        grid=(m // tm, n // tn),
        in_specs=[pl.BlockSpec((tm, k), lambda i, j: (i, 0)),
                  pl.BlockSpec((1, k), lambda i, j: (0, 0)),
                  pl.BlockSpec((k, tn), lambda i, j: (0, j))],
        out_specs=pl.BlockSpec((tm, tn), lambda i, j: (i, j)),
        out_shape=jax.ShapeDtypeStruct((m, n), out_dtype),
        scratch_shapes=[pltpu.VMEM((tm, k), BF16)],
        compiler_params=_params(2),
        name="norm_matmul",
    )(x, gain.reshape(1, k), w)


def _ssd_kernel(xbc_ref, z_ref, dtr_ref, cw_ref, cb_ref, dtb_ref, alog_ref, expand_ref, dskip_ref, gn_ref,
                o_ref, cbuf_ref, state_ref, *, d_inner, heads):
    l = SSD_CHUNK
    gn = SSD_GROUPS * SSD_STATE
    gw = d_inner // SSD_GROUPS
    heads_per_group = heads // SSD_GROUPS

    @pl.when(pl.program_id(1) == 0)
    def _():
        cbuf_ref[0:8, :] = jnp.zeros((8, cbuf_ref.shape[1]), F32)
        state_ref[...] = jnp.zeros(state_ref.shape, F32)

    cbuf_ref[8:8 + l, :] = xbc_ref[0].astype(F32)
    conv = cb_ref[...] + cw_ref[3:4, :] * cbuf_ref[8:8 + l, :]
    for k in range(CONV_WIDTH - 1):
        shift = CONV_WIDTH - 1 - k
        conv = conv + cw_ref[k:k + 1, :] * cbuf_ref[8 - shift:8 - shift + l, :]
    cbuf_ref[0:8, :] = cbuf_ref[l:l + 8, :]
    conv = _silu(conv)
    xs = conv[:, :d_inner]
    bm = conv[:, d_inner:d_inner + gn]
    cm = conv[:, d_inner + gn:]

    dtv = dtr_ref[0] + dtb_ref[...]
    dt = jnp.maximum(dtv, 0.0) + jnp.log1p(jnp.exp(-jnp.abs(dtv)))
    a_dt = dt * (-jnp.exp(alog_ref[...]))
    row_i = lax.broadcasted_iota(jnp.int32, (l, l), 0)
    col_i = lax.broadcasted_iota(jnp.int32, (l, l), 1)
    causal = col_i <= row_i
    tri = causal.astype(BF16)
    hi, lo = _split_bf16(a_dt)
    a_cum = jnp.dot(tri, hi, preferred_element_type=F32) + jnp.dot(tri, lo, preferred_element_type=F32)
    a_cum_t = a_cum.T

    ex = expand_ref[...]

    def expand(v):
        vh, vl = _split_bf16(v)
        return jnp.dot(vh, ex, preferred_element_type=F32) + jnp.dot(vl, ex, preferred_element_type=F32)

    dt_x = expand(dt)
    a_cum_x = expand(a_cum)
    last = a_cum_x[l - 1:l, :]
    grow = jnp.exp(a_cum_x)
    to_end = jnp.exp(last - a_cum_x)
    chunk_decay = jnp.exp(last)

    x_dt = xs * dt_x
    x_dt_b = x_dt.astype(BF16)
    x_end_b = (x_dt * to_end).astype(BF16)

    lane = lax.broadcasted_iota(jnp.int32, (l, LANES), 1)
    first_head = lane < SSD_HEAD_DIM
    outs = []
    for g in range(SSD_GROUPS):
        bg = bm[:, g * SSD_STATE:(g + 1) * SSD_STATE]
        cg = cm[:, g * SSD_STATE:(g + 1) * SSD_STATE].astype(BF16)
        cbm = lax.dot_general(cg, bg.astype(BF16), (((1,), (1,)), ((), ())), preferred_element_type=F32)
        cols = slice(g * gw, (g + 1) * gw)
        diag = []
        for pair in range(heads_per_group // 2):
            h0 = g * heads_per_group + 2 * pair
            xp = x_dt_b[:, h0 * SSD_HEAD_DIM:(h0 + 2) * SSD_HEAD_DIM]
            ys = []
            for h in (h0, h0 + 1):
                seg = a_cum[:, h:h + 1] - a_cum_t[h:h + 1, :]
                mh = (cbm * jnp.exp(jnp.where(causal, seg, NEG))).astype(BF16)
                ys.append(jnp.dot(mh, xp, preferred_element_type=F32))
            diag.append(jnp.where(first_head, ys[0], ys[1]))
        st = state_ref[g]
        y_off = jnp.dot(cg, st.astype(BF16), preferred_element_type=F32) * grow[:, cols]
        state_ref[g] = st * chunk_decay[:, cols] + jnp.dot(bg.T.astype(BF16), x_end_b[:, cols],
                                                          preferred_element_type=F32)
        outs.append(jnp.concatenate(diag, axis=1) + y_off)
    y = jnp.concatenate(outs, axis=1) + dskip_ref[...] * xs

    zz = z_ref[0].astype(F32)
    y = y * _silu(zz)
    normed = []
    for g in range(SSD_GROUPS):
        yg = y[:, g * gw:(g + 1) * gw]
        normed.append(yg * lax.rsqrt(jnp.mean(yg * yg, axis=-1, keepdims=True) + RMS_EPS))
    o_ref[0] = (jnp.concatenate(normed, axis=1) * gn_ref[...]).astype(o_ref.dtype)


def _ssd(xbc, z, dtr, dt_tile, conv_w, conv_b, dt_bias, a_log, d_skip, ssd_norm):
    b, s, xdim = xbc.shape
    d_inner = z.shape[2]
    heads = a_log.shape[0]
    l = SSD_CHUNK
    pad = LANES - heads
    dtb = jnp.pad(dt_bias.astype(F32), (0, pad)).reshape(1, LANES)
    alog = jnp.pad(a_log.astype(F32), (0, pad)).reshape(1, LANES)
    expand = (jnp.arange(LANES)[:, None] == (jnp.arange(d_inner)[None, :] // SSD_HEAD_DIM)).astype(BF16)
    dskip = jnp.repeat(d_skip.astype(F32), SSD_HEAD_DIM).reshape(1, d_inner)
    kern = functools.partial(_ssd_kernel, d_inner=d_inner, heads=heads)
    return pl.pallas_call(
        kern,
        grid=(b, s // l),
        in_specs=[pl.BlockSpec((1, l, xdim), lambda i, c: (i, c, 0)),
                  pl.BlockSpec((1, l, d_inner), lambda i, c: (i, c, 0)),
                  pl.BlockSpec((1, l, LANES), lambda i, c: (i, c, dt_tile)),
                  _const_spec((CONV_WIDTH, xdim)),
                  _const_spec((1, xdim)),
                  _const_spec((1, LANES)),
                  _const_spec((1, LANES)),
                  _const_spec((LANES, d_inner)),
                  _const_spec((1, d_inner)),
                  _const_spec((1, d_inner))],
        out_specs=pl.BlockSpec((1, l, d_inner), lambda i, c: (i, c, 0)),
        out_shape=jax.ShapeDtypeStruct((b, s, d_inner), BF16),
        scratch_shapes=[pltpu.VMEM((l + 8, xdim), F32),
                        pltpu.VMEM((SSD_GROUPS, SSD_STATE, d_inner // SSD_GROUPS), F32)],
        compiler_params=_params(2),
        name="ssd_scan",
    )(xbc, z, dtr, conv_w.astype(F32), conv_b.astype(F32).reshape(1, xdim), dtb, alog, expand, dskip,
      ssd_norm.astype(F32).reshape(1, d_inner))


def _mla_prep_kernel(a_ref, pos_ref, invf_ref, qg_ref, kg_ref, wq1_ref, wq2_ref, wk1_ref, wk2_ref, wv_ref,
                     q_ref, k_ref, v_ref, *, q_rank, kv_rank, scale):
    a = a_ref[...]
    qa = a[:, :q_rank]
    ckv = a[:, q_rank:q_rank + kv_rank]
    rest = a[:, q_rank + kv_rank:]
    qn = (qa * lax.rsqrt(jnp.mean(qa * qa, axis=-1, keepdims=True) + RMS_EPS) * qg_ref[...]).astype(BF16)
    cn = (ckv * lax.rsqrt(jnp.mean(ckv * ckv, axis=-1, keepdims=True) + RMS_EPS) * kg_ref[...]).astype(BF16)
    rest_b = rest.astype(BF16)

    ang = pos_ref[...].astype(F32) * invf_ref[...]
    cos = jnp.concatenate([jnp.cos(ang)] * MLA_HEADS, axis=1)
    sin = jnp.concatenate([jnp.sin(ang)] * MLA_HEADS, axis=1)

    q = jnp.dot(qn, wq1_ref[...], preferred_element_type=F32) * cos
    q = q + jnp.dot(qn, wq2_ref[...], preferred_element_type=F32) * sin
    q_ref[...] = (q * scale).astype(q_ref.dtype)

    kin = jnp.concatenate([cn, rest_b], axis=1)
    k = jnp.dot(kin, wk1_ref[...], preferred_element_type=F32) * cos
    k = k + jnp.dot(rest_b, wk2_ref[...], preferred_element_type=F32) * sin
    k_ref[...] = k.astype(k_ref.dtype)

    v = jnp.dot(cn, wv_ref[...], preferred_element_type=F32)
    lane = lax.broadcasted_iota(jnp.int32, v.shape, 1) % LANES
    v_ref[...] = jnp.where(lane == V_HEAD, 1.0, v).astype(v_ref.dtype)


def _mla_prep(qkv_a, width, positions, q_a_norm, w_q_b, kv_a_norm, w_kv_b, tm):
    t = qkv_a.shape[0]
    q_rank, kv_rank = w_q_b.shape[0], w_kv_b.shape[0]
    hd = MLA_HEADS * LANES
    qk_head = QK_NOPE + QK_ROPE
    half = QK_ROPE // 2
    tm = min(tm, t)

    w3 = w_q_b.astype(F32).reshape(q_rank, MLA_HEADS, qk_head)
    wq1 = jnp.pad(w3, ((0, 0), (0, 0), (0, LANES - qk_head))).reshape(q_rank, hd).astype(BF16)
    pe = w3[..., QK_NOPE:]
    rot = jnp.concatenate([-pe[..., half:], pe[..., :half]], axis=-1)
    wq2 = jnp.pad(rot, ((0, 0), (0, 0), (QK_NOPE, LANES - qk_head))).reshape(q_rank, hd).astype(BF16)

    kv3 = w_kv_b.astype(F32).reshape(kv_rank, MLA_HEADS, QK_NOPE + V_HEAD)
    wk_nope = jnp.pad(kv3[..., :QK_NOPE], ((0, 0), (0, 0), (0, LANES - QK_NOPE))).reshape(kv_rank, hd)
    wv = jnp.pad(kv3[..., QK_NOPE:], ((0, 0), (0, 0), (0, LANES - V_HEAD))).reshape(kv_rank, hd).astype(BF16)
    eye = jnp.eye(QK_ROPE, dtype=F32)
    place = jnp.pad(eye, ((0, LANES - QK_ROPE), (QK_NOPE, LANES - qk_head)))
    eye_rot = jnp.concatenate([-eye[:, half:], eye[:, :half]], axis=1)
    place_rot = jnp.pad(eye_rot, ((0, LANES - QK_ROPE), (QK_NOPE, LANES - qk_head)))
    wk1 = jnp.concatenate([wk_nope, jnp.tile(place, (1, MLA_HEADS))], axis=0).astype(BF16)
    wk2 = jnp.tile(place_rot, (1, MLA_HEADS)).astype(BF16)

    inv_freq = ROPE_THETA ** (-jnp.arange(half, dtype=F32) / half)
    invf = jnp.concatenate([jnp.zeros((QK_NOPE,), F32), inv_freq, inv_freq,
                            jnp.zeros((LANES - qk_head,), F32)]).reshape(1, LANES)

    kern = functools.partial(_mla_prep_kernel, q_rank=q_rank, kv_rank=kv_rank,
                             scale=qk_head ** -0.5 * LOG2_E)
    out = jax.ShapeDtypeStruct((t, hd), BF16)
    return pl.pallas_call(
        kern,
        grid=(t // tm,),
        in_specs=[pl.BlockSpec((tm, width), lambda i: (i, 0)),
                  pl.BlockSpec((tm, 1), lambda i: (i, 0)),
                  _const_spec((1, LANES)),
                  _const_spec((1, q_rank)),
                  _const_spec((1, kv_rank)),
                  _const_spec((q_rank, hd)),
                  _const_spec((q_rank, hd)),
                  _const_spec((kv_rank + LANES, hd)),
                  _const_spec((LANES, hd)),
                  _const_spec((kv_rank, hd))],
        out_specs=[pl.BlockSpec((tm, hd), lambda i: (i, 0))] * 3,
        out_shape=[out, out, out],
        compiler_params=_params(1),
        name="mla_prep",
    )(qkv_a, positions.reshape(t, 1), invf, q_a_norm.astype(F32).reshape(1, q_rank),
      kv_a_norm.astype(F32).reshape(1, kv_rank), wq1, wq2, wk1, wk2, wv)


def _flash_kernel(q_ref, k_ref, v_ref, o_ref, s_ref, m_ref, acc_ref, *, tq, tk, hp):
    i = pl.program_id(2)
    diag_blocks = tq // tk
    assert diag_blocks == 2
    n_tiles = tk // LANES
    m_ref[...] = jnp.full(m_ref.shape, NEG, F32)
    acc_ref[...] = jnp.zeros(acc_ref.shape, F32)

    def rows(j):
        return pl.ds(pl.multiple_of(j * tk, tk), tk)

    def head(h):
        return slice(h * LANES, (h + 1) * LANES)

    def scores(j, slot):
        for h in range(hp):
            s_ref[2 * h + slot] = lax.dot_general(q_ref[0, :, head(h)], k_ref[0, rows(j), head(h)],
                                                  (((1,), (1,)), ((), ())), preferred_element_type=F32)

    def update(j, slot, key_offset=None):
        for h in range(hp):
            tiles = [s_ref[2 * h + slot, :, c * LANES:(c + 1) * LANES] for c in range(n_tiles)]
            if key_offset is not None:
                r = lax.broadcasted_iota(jnp.int32, (tq, LANES), 0)
                c0 = lax.broadcasted_iota(jnp.int32, (tq, LANES), 1)
                tiles = [jnp.where(c0 + (key_offset + c * LANES) <= r, t, NEG) for c, t in enumerate(tiles)]
            m_old = m_ref[h]
            m_new = jnp.maximum(m_old, jnp.max(functools.reduce(jnp.maximum, tiles), axis=-1, keepdims=True))
            p = jnp.concatenate([jnp.exp2(t - m_new) for t in tiles], axis=1).astype(BF16)
            acc_ref[h] = (jnp.exp2(m_old - m_new) * acc_ref[h]
                          + jnp.dot(p, v_ref[0, rows(j), head(h)], preferred_element_type=F32))
            m_ref[h] = m_new

    scores(0, 0)

    def body(jj, carry):
        j = 2 * jj
        scores(j + 1, 1)
        update(j, 0)
        scores(j + 2, 0)
        update(j + 1, 1)
        return carry

    lax.fori_loop(0, i, body, 0)
    scores(2 * i + 1, 1)
    update(2 * i, 0, key_offset=0)
    update(2 * i + 1, 1, key_offset=tk)
    for h in range(hp):
        acc = acc_ref[h]
        o_ref[0, :, head(h)] = (acc / acc[:, V_HEAD:V_HEAD + 1]).astype(o_ref.dtype)


def _flash(q, k, v, tq, tk, hp):
    b, s, hd = q.shape
    w = hp * LANES
    return pl.pallas_call(
        functools.partial(_flash_kernel, tq=tq, tk=tk, hp=hp),
        grid=(b, hd // w, s // tq),
        in_specs=[pl.BlockSpec((1, tq, w), lambda bi, h, i: (bi, i, h)),
                  pl.BlockSpec((1, s, w), lambda bi, h, i: (bi, 0, h)),
                  pl.BlockSpec((1, s, w), lambda bi, h, i: (bi, 0, h))],
        out_specs=pl.BlockSpec((1, tq, w), lambda bi, h, i: (bi, i, h)),
        out_shape=jax.ShapeDtypeStruct((b, s, hd), BF16),
        scratch_shapes=[pltpu.VMEM((2 * hp, tq, tk), F32),
                        pltpu.VMEM((hp, tq, LANES), F32), pltpu.VMEM((hp, tq, LANES), F32)],
        compiler_params=_params(3),
        name="mla_flash",
    )(q, k, v)


def _merge_kernel(yn_ref, o_ref, g_ref, x_ref, wso_ref, wmo_ref, wo_ref, h_ref, *, d_model):
    y_ssd = jnp.dot(yn_ref[...], wso_ref[...], preferred_element_type=F32)
    y_mla = jnp.dot(o_ref[...], wmo_ref[...], preferred_element_type=F32)
    g = g_ref[...].astype(F32)
    merged = _sigmoid(g[:, :d_model]) * y_ssd + _sigmoid(g[:, d_model:]) * y_mla
    h_ref[...] = x_ref[...] + jnp.dot(merged.astype(BF16), wo_ref[...], preferred_element_type=F32)


def _merge(yn, o, gates, x, w_ssd_out, w_mla_pad, w_o, tm):
    t, d_model = x.shape
    tm = min(tm, t)
    row = lambda width: pl.BlockSpec((tm, width), lambda i: (i, 0))
    return pl.pallas_call(
        functools.partial(_merge_kernel, d_model=d_model),
        grid=(t // tm,),
        in_specs=[row(yn.shape[1]), row(o.shape[1]), row(gates.shape[1]), row(d_model),
                  _const_spec(w_ssd_out.shape), _const_spec(w_mla_pad.shape), _const_spec(w_o.shape)],
        out_specs=row(d_model),
        out_shape=jax.ShapeDtypeStruct((t, d_model), F32),
        compiler_params=_params(1),
        name="merge",
    )(yn, o, gates, x, w_ssd_out, w_mla_pad, w_o)


def _xattn_router_kernel(h_ref, kv_ref, gx_ref, gm_ref, wq_ref, wo_ref, wrh_ref, wrl_ref, br_ref,
                         h2_ref, hm_ref, comb_ref, *, d_model):
    h1 = h_ref[...]
    hn = (h1 * lax.rsqrt(jnp.mean(h1 * h1, axis=-1, keepdims=True) + RMS_EPS) * gx_ref[...]).astype(BF16)
    q = jnp.dot(hn, wq_ref[...], preferred_element_type=F32).astype(BF16)
    hd = d_model // XA_HEADS
    kv = kv_ref[0]
    outs = []
    for hh in range(XA_HEADS):
        qh = q[:, hh * hd:(hh + 1) * hd]
        kh = kv[:, hh * hd:(hh + 1) * hd]
        vh = kv[:, d_model + hh * hd:d_model + (hh + 1) * hd]
        s = lax.dot_general(qh, kh, (((1,), (1,)), ((), ())), preferred_element_type=F32) * (hd ** -0.5)
        p = jnp.exp(s - jnp.max(s, axis=-1, keepdims=True))
        p = p / jnp.sum(p, axis=-1, keepdims=True)
        outs.append(jnp.dot(p.astype(BF16), vh, preferred_element_type=F32))
    ox = jnp.concatenate(outs, axis=1).astype(BF16)
    h2 = h1 + jnp.dot(ox, wo_ref[...], preferred_element_type=F32)
    h2_ref[...] = h2

    hm = h2 * lax.rsqrt(jnp.mean(h2 * h2, axis=-1, keepdims=True) + RMS_EPS) * gm_ref[...]
    hm_ref[...] = hm.astype(hm_ref.dtype)

    hi, lo = _split_bf16(hm)
    logits = (jnp.dot(hi, wrh_ref[...], preferred_element_type=F32)
              + jnp.dot(lo, wrh_ref[...], preferred_element_type=F32)
              + jnp.dot(hi, wrl_ref[...], preferred_element_type=F32)) + br_ref[...]
    gl = logits[:, :LANES]
    el = logits[:, LANES:]
    lane = lax.broadcasted_iota(jnp.int32, gl.shape, 1)

    def first_argmax(v, vmax):
        return jnp.min(jnp.where(v == vmax, lane, LANES), axis=-1, keepdims=True)

    gmax = jnp.max(gl, axis=-1, keepdims=True)
    g_sel = first_argmax(gl, gmax)
    g_w = 1.0 / jnp.sum(jnp.exp(gl - gmax), axis=-1, keepdims=True)
    lo_lane = g_sel * EXPERTS_PER_GROUP
    in_group = (lane >= lo_lane) & (lane < lo_lane + EXPERTS_PER_GROUP)
    e1 = jnp.where(in_group, el, NEG)
    v1 = jnp.max(e1, axis=-1, keepdims=True)
    i1 = first_argmax(e1, v1)
    e2 = jnp.where(lane == i1, NEG, e1)
    v2 = jnp.max(e2, axis=-1, keepdims=True)
    i2 = first_argmax(e2, v2)
    r = jnp.exp(v2 - v1)
    w1 = g_w / (1.0 + r)
    w2 = g_w * r / (1.0 + r)
    comb_ref[...] = jnp.where(lane == i1, w1, 0.0) + jnp.where(lane == i2, w2, 0.0)


def _xattn_router(h1, kvx, tokens_per_batch, norm_xattn, norm_moe, w_xq, w_xo, wr_hi, wr_lo, b_r, tm):
    t, d_model = h1.shape
    tm = min(tm, tokens_per_batch)
    per_b = tokens_per_batch // tm
    n_mem = kvx.shape[1]
    row = lambda width: pl.BlockSpec((tm, width), lambda i: (i, 0))
    return pl.pallas_call(
        functools.partial(_xattn_router_kernel, d_model=d_model),
        grid=(t // tm,),
        in_specs=[row(d_model),
                  pl.BlockSpec((1, n_mem, 2 * d_model), lambda i: (i // per_b, 0, 0)),
                  _const_spec((1, d_model)), _const_spec((1, d_model)),
                  _const_spec(w_xq.shape), _const_spec(w_xo.shape),
                  _const_spec(wr_hi.shape), _const_spec(wr_lo.shape), _const_spec(b_r.shape)],
        out_specs=[row(d_model), row(d_model), row(LANES)],
        out_shape=[jax.ShapeDtypeStruct((t, d_model), F32),
                   jax.ShapeDtypeStruct((t, d_model), BF16),
                   jax.ShapeDtypeStruct((t, LANES), F32)],
        compiler_params=_params(1),
        name="xattn_router",
    )(h1, kvx, norm_xattn.astype(F32).reshape(1, d_model), norm_moe.astype(F32).reshape(1, d_model),
      w_xq, w_xo, wr_hi, wr_lo, b_r)


def _moe_kernel(hm_ref, comb_ref, h2_ref, wg_ref, wu_ref, wd_ref, gf_ref, o_ref, acc_ref, *, final_norm):
    e = pl.program_id(1)

    @pl.when(e == 0)
    def _():
        acc_ref[...] = jnp.zeros(acc_ref.shape, F32)

    t = hm_ref[...]
    comb = comb_ref[...]
    lane = lax.broadcasted_iota(jnp.int32, comb.shape, 1)
    w = jnp.sum(jnp.where(lane == e, comb, 0.0), axis=-1, keepdims=True)
    hg = jnp.dot(t, wg_ref[0], preferred_element_type=F32)
    hu = jnp.dot(t, wu_ref[0], preferred_element_type=F32)
    act = (_silu(hg) * hu * w).astype(BF16)
    acc_ref[...] += jnp.dot(act, wd_ref[0], preferred_element_type=F32)

    @pl.when(e == pl.num_programs(1) - 1)
    def _():
        h3 = h2_ref[...] + acc_ref[...]
        if final_norm:
            h3 = h3 * lax.rsqrt(jnp.mean(h3 * h3, axis=-1, keepdims=True) + RMS_EPS) * gf_ref[...]
        o_ref[...] = h3


def _moe(hm, comb, h2, w_gate, w_up, w_down, norm_final, final_norm, tm):
    t, d_model = hm.shape
    n_exp, _, d_exp = w_gate.shape
    tm = min(tm, t)
    row = lambda width: pl.BlockSpec((tm, width), lambda i, e: (i, 0))
    return pl.pallas_call(
        functools.partial(_moe_kernel, final_norm=final_norm),
        grid=(t // tm, n_exp),
        in_specs=[row(d_model), row(LANES), row(d_model),
                  pl.BlockSpec((1, d_model, d_exp), lambda i, e: (e, 0, 0)),
                  pl.BlockSpec((1, d_model, d_exp), lambda i, e: (e, 0, 0)),
                  pl.BlockSpec((1, d_exp, d_model), lambda i, e: (e, 0, 0)),
                  pl.BlockSpec((1, d_model), lambda i, e: (0, 0))],
        out_specs=row(d_model),
        out_shape=jax.ShapeDtypeStruct((t, d_model), F32),
        scratch_shapes=[pltpu.VMEM((tm, d_model), F32)],
        compiler_params=_params(2),
        name="moe_experts",
    )(hm, comb, h2, w_gate, w_up, w_down, norm_final.astype(F32).reshape(1, d_model))


def kernel(x, mem, positions, norm_mix, w_in, conv_w, conv_b, dt_bias, a_log, d_skip, ssd_norm, w_ssd_out, q_a_norm, w_q_b, kv_a_norm, w_kv_b, w_mla_out, w_o, norm_xattn, norm_mem, w_xq, w_xkv, w_xo, norm_moe, w_router_group, b_router_group, w_router_expert, b_router_expert, w_exp_gate, w_exp_up, w_exp_down, norm_final):
    b, s, d_model = x.shape
    t = b * s
    n_mem = mem.shape[1]
    depth = norm_mix.shape[0]
    d_inner = ssd_norm.shape[1]
    xbc_dim = conv_w.shape[2]
    heads = a_log.shape[1]
    q_rank, kv_rank = w_q_b.shape[1], w_kv_b.shape[1]
    n_groups = w_router_group.shape[2]
    n_exp = w_router_expert.shape[2]

    o_z, o_xbc = 0, d_inner
    o_dt = o_xbc + xbc_dim
    o_qa = o_dt + heads
    o_gs = o_qa + q_rank + kv_rank + QK_ROPE
    o_end = o_gs + 2 * d_model

    h = x.reshape(t, d_model).astype(F32)
    for i in range(depth):
        wi = w_in[i]
        w_z = wi[:, o_z:o_xbc].astype(BF16)
        w_xbc = wi[:, o_xbc:o_dt].astype(BF16)
        qkv_width = q_rank + kv_rank + LANES
        w_small = jnp.concatenate([jnp.pad(wi[:, o_qa:o_gs], ((0, 0), (0, LANES - QK_ROPE))),
                                   jnp.pad(wi[:, o_dt:o_qa], ((0, 0), (0, LANES - heads)))], axis=1).astype(BF16)
        w_gates = wi[:, o_gs:o_end].astype(BF16)

        zb = _norm_matmul(h, norm_mix[i], w_z, BF16, 1024, 1024).reshape(b, s, d_inner)
        xbc = _norm_matmul(h, norm_mix[i], w_xbc, BF16, 1024, 1024).reshape(b, s, xbc_dim)
        gates = _norm_matmul(h, norm_mix[i], w_gates, BF16, 1024, 1024)
        small = _norm_matmul(h, norm_mix[i], w_small, F32, 1024, w_small.shape[1])

        yn = _ssd(xbc, zb, small.reshape(b, s, w_small.shape[1]), qkv_width // LANES, conv_w[i], conv_b[i],
                  dt_bias[i], a_log[i], d_skip[i], ssd_norm[i])

        hd = MLA_HEADS * LANES
        q, k, v = _mla_prep(small, qkv_width, positions, q_a_norm[i], w_q_b[i], kv_a_norm[i], w_kv_b[i], 512)
        o = _flash(q.reshape(b, s, hd), k.reshape(b, s, hd), v.reshape(b, s, hd), FLASH_TQ, FLASH_TK, FLASH_HP)

        w_mla_pad = jnp.pad(w_mla_out[i].reshape(MLA_HEADS, V_HEAD, d_model),
                            ((0, 0), (0, LANES - V_HEAD), (0, 0))).reshape(hd, d_model).astype(BF16)
        h1 = _merge(yn.reshape(t, d_inner), o.reshape(t, hd), gates, h, w_ssd_out[i].astype(BF16), w_mla_pad,
                    w_o[i].astype(BF16), 256)

        kvx = _norm_matmul(mem.reshape(b * n_mem, d_model).astype(F32), norm_mem[i], w_xkv[i].astype(BF16), BF16,
                           b * n_mem, 1024).reshape(b, n_mem, 2 * d_model)
        w_r = jnp.concatenate([jnp.pad(w_router_group[i].astype(F32), ((0, 0), (0, LANES - n_groups))),
                               jnp.pad(w_router_expert[i].astype(F32), ((0, 0), (0, LANES - n_exp)))], axis=1)
        wr_hi = w_r.astype(BF16)
        wr_lo = (w_r - wr_hi.astype(F32)).astype(BF16)
        b_r = jnp.concatenate([jnp.pad(b_router_group[i].astype(F32), (0, LANES - n_groups), constant_values=NEG),
                               jnp.pad(b_router_expert[i].astype(F32), (0, LANES - n_exp), constant_values=NEG)]
                              ).reshape(1, 2 * LANES)
        h2, hm, comb = _xattn_router(h1, kvx, s, norm_xattn[i], norm_moe[i], w_xq[i].astype(BF16),
                                     w_xo[i].astype(BF16), wr_hi, wr_lo, b_r, 512)

        h = _moe(hm, comb, h2, w_exp_gate[i].astype(BF16), w_exp_up[i].astype(BF16), w_exp_down[i].astype(BF16),
                 norm_final, i == depth - 1, 1024)
    return h.reshape(b, s, d_model)
```

```python
import functools

import jax
import jax.numpy as jnp
from jax import lax
from jax.experimental import pallas as pl
from jax.experimental.pallas import tpu as pltpu

F32 = jnp.float32
BF16 = jnp.bfloat16
RMS_EPS = 1e-6
NEG = -1e30
ROPE_THETA = 10000.0
LOG2_E = 1.4426950408889634

LANES = 128
SSD_CHUNK = 128
SSD_HEAD_DIM = 64
SSD_GROUPS = 4
SSD_STATE = 128
CONV_WIDTH = 4
MLA_HEADS = 16
QK_NOPE = 64
QK_ROPE = 32
V_HEAD = 64
XA_HEADS = 4
EXPERTS_PER_GROUP = 8
VMEM_LIMIT = 56 * 1024 * 1024
FLASH_TQ = 1024
FLASH_TK = 512
FLASH_HP = 2
MOE_TM = 1024
MOE_CAP = 384


def _params(n_axes):
    return pltpu.CompilerParams(dimension_semantics=("arbitrary",) * n_axes,
                                vmem_limit_bytes=VMEM_LIMIT)


def _sigmoid(v):
    return 1.0 / (1.0 + jnp.exp(-v))


def _silu(v):
    return v * _sigmoid(v)


def _split_bf16(v):
    hi = v.astype(BF16)
    lo = (v - hi.astype(F32)).astype(BF16)
    return hi, lo


def _const_spec(shape):
    return pl.BlockSpec(shape, lambda *_: (0,) * len(shape))


def _norm_matmul_kernel(x_ref, g_ref, w_ref, o_ref, un_ref):
    @pl.when(pl.program_id(1) == 0)
    def _():
        x = x_ref[...]
        r = lax.rsqrt(jnp.mean(x * x, axis=-1, keepdims=True) + RMS_EPS)
        un_ref[...] = (x * r * g_ref[...]).astype(BF16)

    o_ref[...] = jnp.dot(un_ref[...], w_ref[...], preferred_element_type=F32).astype(o_ref.dtype)


def _norm_matmul(x, gain, w, out_dtype, tm, tn):
    m, k = x.shape
    n = w.shape[1]
    tm, tn = min(tm, m), min(tn, n)
    return pl.pallas_call(
        _norm_matmul_kernel,
        grid=(m // tm, n // tn),
        in_specs=[pl.BlockSpec((tm, k), lambda i, j: (i, 0)),
                  pl.BlockSpec((1, k), lambda i, j: (0, 0)),
                  pl.BlockSpec((k, tn), lambda i, j: (0, j))],
        out_specs=pl.BlockSpec((tm, tn), lambda i, j: (i, j)),
        out_shape=jax.ShapeDtypeStruct((m, n), out_dtype),
        scratch_shapes=[pltpu.VMEM((tm, k), BF16)],
        compiler_params=_params(2),
        name="norm_matmul",
    )(x, gain.reshape(1, k), w)


def _ssd_kernel(xbc_ref, z_ref, dtr_ref, cw_ref, cb_ref, dtb_ref, alog_ref, expand_ref, dskip_ref, gn_ref,
                o_ref, cbuf_ref, state_ref, *, d_inner, heads):
    l = SSD_CHUNK
    gn = SSD_GROUPS * SSD_STATE
    gw = d_inner // SSD_GROUPS
    heads_per_group = heads // SSD_GROUPS

    @pl.when(pl.program_id(1) == 0)
    def _():
        cbuf_ref[0:8, :] = jnp.zeros((8, cbuf_ref.shape[1]), F32)
        state_ref[...] = jnp.zeros(state_ref.shape, F32)

    cbuf_ref[8:8 + l, :] = xbc_ref[0].astype(F32)
    conv = cb_ref[...] + cw_ref[3:4, :] * cbuf_ref[8:8 + l, :]
    for k in range(CONV_WIDTH - 1):
        shift = CONV_WIDTH - 1 - k
        conv = conv + cw_ref[k:k + 1, :] * cbuf_ref[8 - shift:8 - shift + l, :]
    cbuf_ref[0:8, :] = cbuf_ref[l:l + 8, :]
    conv = _silu(conv)
    xs = conv[:, :d_inner]
    bm = conv[:, d_inner:d_inner + gn]
    cm = conv[:, d_inner + gn:]

    dtv = dtr_ref[0] + dtb_ref[...]
    dt = jnp.maximum(dtv, 0.0) + jnp.log1p(jnp.exp(-jnp.abs(dtv)))
    a_dt = dt * (-jnp.exp(alog_ref[...]))
    row_i = lax.broadcasted_iota(jnp.int32, (l, l), 0)
    col_i = lax.broadcasted_iota(jnp.int32, (l, l), 1)
    causal = col_i <= row_i
    tri = causal.astype(BF16)
    hi, lo = _split_bf16(a_dt)
    a_cum = jnp.dot(tri, hi, preferred_element_type=F32) + jnp.dot(tri, lo, preferred_element_type=F32)
    a_cum_t = a_cum.T

    ex = expand_ref[...]

    def expand(v):
        vh, vl = _split_bf16(v)
        return jnp.dot(vh, ex, preferred_element_type=F32) + jnp.dot(vl, ex, preferred_element_type=F32)

    dt_x = expand(dt)
    a_cum_x = expand(a_cum)
    last = a_cum_x[l - 1:l, :]
    grow = jnp.exp(a_cum_x)
    to_end = jnp.exp(last - a_cum_x)
    chunk_decay = jnp.exp(last)

    x_dt = xs * dt_x
    x_dt_b = x_dt.astype(BF16)
    x_end_b = (x_dt * to_end).astype(BF16)

    lane = lax.broadcasted_iota(jnp.int32, (l, LANES), 1)
    first_head = lane < SSD_HEAD_DIM
    outs = []
    for g in range(SSD_GROUPS):
        bg = bm[:, g * SSD_STATE:(g + 1) * SSD_STATE]
        cg = cm[:, g * SSD_STATE:(g + 1) * SSD_STATE].astype(BF16)
        cbm = lax.dot_general(cg, bg.astype(BF16), (((1,), (1,)), ((), ())), preferred_element_type=F32)
        cols = slice(g * gw, (g + 1) * gw)
        diag = []
        for pair in range(heads_per_group // 2):
            h0 = g * heads_per_group + 2 * pair
            xp = x_dt_b[:, h0 * SSD_HEAD_DIM:(h0 + 2) * SSD_HEAD_DIM]
            ys = []
            for h in (h0, h0 + 1):
                seg = a_cum[:, h:h + 1] - a_cum_t[h:h + 1, :]
                mh = (cbm * jnp.exp(jnp.where(causal, seg, NEG))).astype(BF16)
                ys.append(jnp.dot(mh, xp, preferred_element_type=F32))
            diag.append(jnp.where(first_head, ys[0], ys[1]))
        st = state_ref[g]
        y_off = jnp.dot(cg, st.astype(BF16), preferred_element_type=F32) * grow[:, cols]
        state_ref[g] = st * chunk_decay[:, cols] + jnp.dot(bg.T.astype(BF16), x_end_b[:, cols],
                                                          preferred_element_type=F32)
        outs.append(jnp.concatenate(diag, axis=1) + y_off)
    y = jnp.concatenate(outs, axis=1) + dskip_ref[...] * xs

    zz = z_ref[0].astype(F32)
    y = y * _silu(zz)
    normed = []
    for g in range(SSD_GROUPS):
        yg = y[:, g * gw:(g + 1) * gw]
        normed.append(yg * lax.rsqrt(jnp.mean(yg * yg, axis=-1, keepdims=True) + RMS_EPS))
    o_ref[0] = (jnp.concatenate(normed, axis=1) * gn_ref[...]).astype(o_ref.dtype)


def _ssd(xbc, z, dtr, dt_tile, conv_w, conv_b, dt_bias, a_log, d_skip, ssd_norm):
    b, s, xdim = xbc.shape
    d_inner = z.shape[2]
    heads = a_log.shape[0]
    l = SSD_CHUNK
    pad = LANES - heads
    dtb = jnp.pad(dt_bias.astype(F32), (0, pad)).reshape(1, LANES)
    alog = jnp.pad(a_log.astype(F32), (0, pad)).reshape(1, LANES)
    expand = (jnp.arange(LANES)[:, None] == (jnp.arange(d_inner)[None, :] // SSD_HEAD_DIM)).astype(BF16)
    dskip = jnp.repeat(d_skip.astype(F32), SSD_HEAD_DIM).reshape(1, d_inner)
    kern = functools.partial(_ssd_kernel, d_inner=d_inner, heads=heads)
    return pl.pallas_call(
        kern,
        grid=(b, s // l),
        in_specs=[pl.BlockSpec((1, l, xdim), lambda i, c: (i, c, 0)),
                  pl.BlockSpec((1, l, d_inner), lambda i, c: (i, c, 0)),
                  pl.BlockSpec((1, l, LANES), lambda i, c: (i, c, dt_tile)),
                  _const_spec((CONV_WIDTH, xdim)),
                  _const_spec((1, xdim)),
                  _const_spec((1, LANES)),
                  _const_spec((1, LANES)),
                  _const_spec((LANES, d_inner)),
                  _const_spec((1, d_inner)),
                  _const_spec((1, d_inner))],
        out_specs=pl.BlockSpec((1, l, d_inner), lambda i, c: (i, c, 0)),
        out_shape=jax.ShapeDtypeStruct((b, s, d_inner), BF16),
        scratch_shapes=[pltpu.VMEM((l + 8, xdim), F32),
                        pltpu.VMEM((SSD_GROUPS, SSD_STATE, d_inner // SSD_GROUPS), F32)],
        compiler_params=_params(2),
        name="ssd_scan",
    )(xbc, z, dtr, conv_w.astype(F32), conv_b.astype(F32).reshape(1, xdim), dtb, alog, expand, dskip,
      ssd_norm.astype(F32).reshape(1, d_inner))


def _mla_prep_kernel(a_ref, pos_ref, invf_ref, qg_ref, kg_ref, wq1_ref, wq2_ref, wk1_ref, wk2_ref, wv_ref,
                     q_ref, k_ref, v_ref, *, q_rank, kv_rank, scale):
    a = a_ref[...]
    qa = a[:, :q_rank]
    ckv = a[:, q_rank:q_rank + kv_rank]
    rest = a[:, q_rank + kv_rank:]
    qn = (qa * lax.rsqrt(jnp.mean(qa * qa, axis=-1, keepdims=True) + RMS_EPS) * qg_ref[...]).astype(BF16)
    cn = (ckv * lax.rsqrt(jnp.mean(ckv * ckv, axis=-1, keepdims=True) + RMS_EPS) * kg_ref[...]).astype(BF16)
    rest_b = rest.astype(BF16)

    ang = pos_ref[...].astype(F32) * invf_ref[...]
    cos = jnp.concatenate([jnp.cos(ang)] * MLA_HEADS, axis=1)
    sin = jnp.concatenate([jnp.sin(ang)] * MLA_HEADS, axis=1)

    q = jnp.dot(qn, wq1_ref[...], preferred_element_type=F32) * cos
    q = q + jnp.dot(qn, wq2_ref[...], preferred_element_type=F32) * sin
    q_ref[...] = (q * scale).astype(q_ref.dtype)

    kin = jnp.concatenate([cn, rest_b], axis=1)
    k = jnp.dot(kin, wk1_ref[...], preferred_element_type=F32) * cos
    k = k + jnp.dot(rest_b, wk2_ref[...], preferred_element_type=F32) * sin
    k_ref[...] = k.astype(k_ref.dtype)

    v = jnp.dot(cn, wv_ref[...], preferred_element_type=F32)
    lane = lax.broadcasted_iota(jnp.int32, v.shape, 1) % LANES
    v_ref[...] = jnp.where(lane == V_HEAD, 1.0, v).astype(v_ref.dtype)


def _mla_prep(qkv_a, width, positions, q_a_norm, w_q_b, kv_a_norm, w_kv_b, tm):
    t = qkv_a.shape[0]
    q_rank, kv_rank = w_q_b.shape[0], w_kv_b.shape[0]
    hd = MLA_HEADS * LANES
    qk_head = QK_NOPE + QK_ROPE
    half = QK_ROPE // 2
    tm = min(tm, t)

    w3 = w_q_b.astype(F32).reshape(q_rank, MLA_HEADS, qk_head)
    wq1 = jnp.pad(w3, ((0, 0), (0, 0), (0, LANES - qk_head))).reshape(q_rank, hd).astype(BF16)
    pe = w3[..., QK_NOPE:]
    rot = jnp.concatenate([-pe[..., half:], pe[..., :half]], axis=-1)
    wq2 = jnp.pad(rot, ((0, 0), (0, 0), (QK_NOPE, LANES - qk_head))).reshape(q_rank, hd).astype(BF16)

    kv3 = w_kv_b.astype(F32).reshape(kv_rank, MLA_HEADS, QK_NOPE + V_HEAD)
    wk_nope = jnp.pad(kv3[..., :QK_NOPE], ((0, 0), (0, 0), (0, LANES - QK_NOPE))).reshape(kv_rank, hd)
    wv = jnp.pad(kv3[..., QK_NOPE:], ((0, 0), (0, 0), (0, LANES - V_HEAD))).reshape(kv_rank, hd).astype(BF16)
    eye = jnp.eye(QK_ROPE, dtype=F32)
    place = jnp.pad(eye, ((0, LANES - QK_ROPE), (QK_NOPE, LANES - qk_head)))
    eye_rot = jnp.concatenate([-eye[:, half:], eye[:, :half]], axis=1)
    place_rot = jnp.pad(eye_rot, ((0, LANES - QK_ROPE), (QK_NOPE, LANES - qk_head)))
    wk1 = jnp.concatenate([wk_nope, jnp.tile(place, (1, MLA_HEADS))], axis=0).astype(BF16)
    wk2 = jnp.tile(place_rot, (1, MLA_HEADS)).astype(BF16)

    inv_freq = ROPE_THETA ** (-jnp.arange(half, dtype=F32) / half)
    invf = jnp.concatenate([jnp.zeros((QK_NOPE,), F32), inv_freq, inv_freq,
                            jnp.zeros((LANES - qk_head,), F32)]).reshape(1, LANES)

    kern = functools.partial(_mla_prep_kernel, q_rank=q_rank, kv_rank=kv_rank,
                             scale=qk_head ** -0.5 * LOG2_E)
    out = jax.ShapeDtypeStruct((t, hd), BF16)
    return pl.pallas_call(
        kern,
        grid=(t // tm,),
        in_specs=[pl.BlockSpec((tm, width), lambda i: (i, 0)),
                  pl.BlockSpec((tm, 1), lambda i: (i, 0)),
                  _const_spec((1, LANES)),
                  _const_spec((1, q_rank)),
                  _const_spec((1, kv_rank)),
                  _const_spec((q_rank, hd)),
                  _const_spec((q_rank, hd)),
                  _const_spec((kv_rank + LANES, hd)),
                  _const_spec((LANES, hd)),
                  _const_spec((kv_rank, hd))],
        out_specs=[pl.BlockSpec((tm, hd), lambda i: (i, 0))] * 3,
        out_shape=[out, out, out],
        compiler_params=_params(1),
        name="mla_prep",
    )(qkv_a, positions.reshape(t, 1), invf, q_a_norm.astype(F32).reshape(1, q_rank),
      kv_a_norm.astype(F32).reshape(1, kv_rank), wq1, wq2, wk1, wk2, wv)


def _flash_kernel(q_ref, k_ref, v_ref, o_ref, s_ref, m_ref, acc_ref, *, tq, tk, hp):
    i = pl.program_id(2)
    diag_blocks = tq // tk
    assert diag_blocks == 2
    n_tiles = tk // LANES
    m_ref[...] = jnp.full(m_ref.shape, NEG, F32)
    acc_ref[...] = jnp.zeros(acc_ref.shape, F32)

    def rows(j):
        return pl.ds(pl.multiple_of(j * tk, tk), tk)

    def head(h):
        return slice(h * LANES, (h + 1) * LANES)

    def scores(j, slot):
        for h in range(hp):
            s_ref[2 * h + slot] = lax.dot_general(q_ref[0, :, head(h)], k_ref[0, rows(j), head(h)],
                                                  (((1,), (1,)), ((), ())), preferred_element_type=F32)

    def update(j, slot, key_offset=None):
        for h in range(hp):
            tiles = [s_ref[2 * h + slot, :, c * LANES:(c + 1) * LANES] for c in range(n_tiles)]
            if key_offset is not None:
                r = lax.broadcasted_iota(jnp.int32, (tq, LANES), 0)
                c0 = lax.broadcasted_iota(jnp.int32, (tq, LANES), 1)
                tiles = [jnp.where(c0 + (key_offset + c * LANES) <= r, t, NEG) for c, t in enumerate(tiles)]
            m_old = m_ref[h]
            m_new = jnp.maximum(m_old, jnp.max(functools.reduce(jnp.maximum, tiles), axis=-1, keepdims=True))
            p = jnp.concatenate([jnp.exp2(t - m_new) for t in tiles], axis=1).astype(BF16)
            acc_ref[h] = (jnp.exp2(m_old - m_new) * acc_ref[h]
                          + jnp.dot(p, v_ref[0, rows(j), head(h)], preferred_element_type=F32))
            m_ref[h] = m_new

    scores(0, 0)

    def body(jj, carry):
        j = 2 * jj
        scores(j + 1, 1)
        update(j, 0)
        scores(j + 2, 0)
        update(j + 1, 1)
        return carry

    lax.fori_loop(0, i, body, 0)
    scores(2 * i + 1, 1)
    update(2 * i, 0, key_offset=0)
    update(2 * i + 1, 1, key_offset=tk)
    for h in range(hp):
        acc = acc_ref[h]
        o_ref[0, :, head(h)] = (acc / acc[:, V_HEAD:V_HEAD + 1]).astype(o_ref.dtype)


def _flash(q, k, v, tq, tk, hp):
    b, s, hd = q.shape
    w = hp * LANES
    return pl.pallas_call(
        functools.partial(_flash_kernel, tq=tq, tk=tk, hp=hp),
        grid=(b, hd // w, s // tq),
        in_specs=[pl.BlockSpec((1, tq, w), lambda bi, h, i: (bi, i, h)),
                  pl.BlockSpec((1, s, w), lambda bi, h, i: (bi, 0, h)),
                  pl.BlockSpec((1, s, w), lambda bi, h, i: (bi, 0, h))],
        out_specs=pl.BlockSpec((1, tq, w), lambda bi, h, i: (bi, i, h)),
        out_shape=jax.ShapeDtypeStruct((b, s, hd), BF16),
        scratch_shapes=[pltpu.VMEM((2 * hp, tq, tk), F32),
                        pltpu.VMEM((hp, tq, LANES), F32), pltpu.VMEM((hp, tq, LANES), F32)],
        compiler_params=_params(3),
        name="mla_flash",
    )(q, k, v)


def _merge_kernel(yn_ref, o_ref, g_ref, x_ref, wso_ref, wmo_ref, wo_ref, h_ref, *, d_model):
    y_ssd = jnp.dot(yn_ref[...], wso_ref[...], preferred_element_type=F32)
    y_mla = jnp.dot(o_ref[...], wmo_ref[...], preferred_element_type=F32)
    g = g_ref[...].astype(F32)
    merged = _sigmoid(g[:, :d_model]) * y_ssd + _sigmoid(g[:, d_model:]) * y_mla
    h_ref[...] = x_ref[...] + jnp.dot(merged.astype(BF16), wo_ref[...], preferred_element_type=F32)


def _merge(yn, o, gates, x, w_ssd_out, w_mla_pad, w_o, tm):
    t, d_model = x.shape
    tm = min(tm, t)
    row = lambda width: pl.BlockSpec((tm, width), lambda i: (i, 0))
    return pl.pallas_call(
        functools.partial(_merge_kernel, d_model=d_model),
        grid=(t // tm,),
        in_specs=[row(yn.shape[1]), row(o.shape[1]), row(gates.shape[1]), row(d_model),
                  _const_spec(w_ssd_out.shape), _const_spec(w_mla_pad.shape), _const_spec(w_o.shape)],
        out_specs=row(d_model),
        out_shape=jax.ShapeDtypeStruct((t, d_model), F32),
        compiler_params=_params(1),
        name="merge",
    )(yn, o, gates, x, w_ssd_out, w_mla_pad, w_o)


def _xattn_router_kernel(h_ref, kv_ref, gx_ref, gm_ref, wq_ref, wo_ref, wrh_ref, wrl_ref, br_ref,
                         h2_ref, hm_ref, comb_ref, *, d_model):
    h1 = h_ref[...]
    hn = (h1 * lax.rsqrt(jnp.mean(h1 * h1, axis=-1, keepdims=True) + RMS_EPS) * gx_ref[...]).astype(BF16)
    q = jnp.dot(hn, wq_ref[...], preferred_element_type=F32).astype(BF16)
    hd = d_model // XA_HEADS
    kv = kv_ref[0]
    outs = []
    for hh in range(XA_HEADS):
        qh = q[:, hh * hd:(hh + 1) * hd]
        kh = kv[:, hh * hd:(hh + 1) * hd]
        vh = kv[:, d_model + hh * hd:d_model + (hh + 1) * hd]
        s = lax.dot_general(qh, kh, (((1,), (1,)), ((), ())), preferred_element_type=F32) * (hd ** -0.5)
        p = jnp.exp(s - jnp.max(s, axis=-1, keepdims=True))
        p = p / jnp.sum(p, axis=-1, keepdims=True)
        outs.append(jnp.dot(p.astype(BF16), vh, preferred_element_type=F32))
    ox = jnp.concatenate(outs, axis=1).astype(BF16)
    h2 = h1 + jnp.dot(ox, wo_ref[...], preferred_element_type=F32)
    h2_ref[...] = h2

    hm = h2 * lax.rsqrt(jnp.mean(h2 * h2, axis=-1, keepdims=True) + RMS_EPS) * gm_ref[...]
    hm_ref[...] = hm.astype(hm_ref.dtype)

    hi, lo = _split_bf16(hm)
    logits = (jnp.dot(hi, wrh_ref[...], preferred_element_type=F32)
              + jnp.dot(lo, wrh_ref[...], preferred_element_type=F32)
              + jnp.dot(hi, wrl_ref[...], preferred_element_type=F32)) + br_ref[...]
    gl = logits[:, :LANES]
    el = logits[:, LANES:]
    lane = lax.broadcasted_iota(jnp.int32, gl.shape, 1)

    def first_argmax(v, vmax):
        return jnp.min(jnp.where(v == vmax, lane, LANES), axis=-1, keepdims=True)

    gmax = jnp.max(gl, axis=-1, keepdims=True)
    g_sel = first_argmax(gl, gmax)
    g_w = 1.0 / jnp.sum(jnp.exp(gl - gmax), axis=-1, keepdims=True)
    lo_lane = g_sel * EXPERTS_PER_GROUP
    in_group = (lane >= lo_lane) & (lane < lo_lane + EXPERTS_PER_GROUP)
    e1 = jnp.where(in_group, el, NEG)
    v1 = jnp.max(e1, axis=-1, keepdims=True)
    i1 = first_argmax(e1, v1)
    e2 = jnp.where(lane == i1, NEG, e1)
    v2 = jnp.max(e2, axis=-1, keepdims=True)
    i2 = first_argmax(e2, v2)
    r = jnp.exp(v2 - v1)
    w1 = g_w / (1.0 + r)
    w2 = g_w * r / (1.0 + r)
    comb_ref[...] = jnp.where(lane == i1, w1, 0.0) + jnp.where(lane == i2, w2, 0.0)


def _xattn_router(h1, kvx, tokens_per_batch, norm_xattn, norm_moe, w_xq, w_xo, wr_hi, wr_lo, b_r, tm):
    t, d_model = h1.shape
    tm = min(tm, tokens_per_batch)
    per_b = tokens_per_batch // tm
    n_mem = kvx.shape[1]
    row = lambda width: pl.BlockSpec((tm, width), lambda i: (i, 0))
    return pl.pallas_call(
        functools.partial(_xattn_router_kernel, d_model=d_model),
        grid=(t // tm,),
        in_specs=[row(d_model),
                  pl.BlockSpec((1, n_mem, 2 * d_model), lambda i: (i // per_b, 0, 0)),
                  _const_spec((1, d_model)), _const_spec((1, d_model)),
                  _const_spec(w_xq.shape), _const_spec(w_xo.shape),
                  _const_spec(wr_hi.shape), _const_spec(wr_lo.shape), _const_spec(b_r.shape)],
        out_specs=[row(d_model), row(d_model), row(LANES)],
        out_shape=[jax.ShapeDtypeStruct((t, d_model), F32),
                   jax.ShapeDtypeStruct((t, d_model), BF16),
                   jax.ShapeDtypeStruct((t, LANES), F32)],
        compiler_params=_params(1),
        name="xattn_router",
    )(h1, kvx, norm_xattn.astype(F32).reshape(1, d_model), norm_moe.astype(F32).reshape(1, d_model),
      w_xq, w_xo, wr_hi, wr_lo, b_r)


def _moe_kernel(hm_ref, comb_ref, h2_ref, upper_ref, gmap_ref, wg_ref, wu_ref, wd_ref, gf_ref, o_ref,
                acc_ref, mem_ref, rank_ref, mem_t_ref, rank_t_ref, *, final_norm, cap):
    g = pl.program_id(1)
    tm = hm_ref.shape[0]
    n_local = wg_ref.shape[0]
    comb = comb_ref[...]
    lane = lax.broadcasted_iota(jnp.int32, (tm, LANES), 1)

    @pl.when(g == 0)
    def _():
        acc_ref[...] = jnp.zeros(acc_ref.shape, F32)
        chosen = (comb > 0.0).astype(BF16)
        member = (jnp.dot(chosen, gmap_ref[...], preferred_element_type=F32) > 0.0).astype(F32)
        member_t = member.T
        rank_t = jnp.dot(member_t.astype(BF16), upper_ref[...], preferred_element_type=F32)
        mem_ref[...] = member
        mem_t_ref[...] = member_t
        rank_t_ref[...] = rank_t
        rank_ref[...] = rank_t.T

    mem_row = mem_t_ref[pl.ds(g, 1), :]
    rank_row = rank_t_ref[pl.ds(g, 1), :]
    pick = lane == g
    mem_col = jnp.sum(jnp.where(pick, mem_ref[...], 0.0), axis=-1, keepdims=True)
    rank_col = jnp.sum(jnp.where(pick, rank_ref[...], 0.0), axis=-1, keepdims=True)
    count = jnp.sum(mem_row).astype(jnp.int32)
    comb_hi, comb_lo = _split_bf16(comb)
    hm = hm_ref[...]

    def chunk(c, carry):
        base = (c * cap).astype(F32)
        slot_r = lax.broadcasted_iota(jnp.int32, (cap, tm), 0).astype(F32) + base
        sel = ((rank_row == slot_r) & (mem_row > 0.0)).astype(BF16)
        xc = jnp.dot(sel, hm, preferred_element_type=F32).astype(BF16)
        wc = (jnp.dot(sel, comb_hi, preferred_element_type=F32)
              + jnp.dot(sel, comb_lo, preferred_element_type=F32))
        lane_c = lax.broadcasted_iota(jnp.int32, (cap, LANES), 1)
        yc = jnp.zeros((cap, hm.shape[1]), F32)
        for e in range(n_local):
            w = jnp.sum(jnp.where(lane_c == g * n_local + e, wc, 0.0), axis=-1, keepdims=True)
            hg = jnp.dot(xc, wg_ref[e], preferred_element_type=F32)
            hu = jnp.dot(xc, wu_ref[e], preferred_element_type=F32)
            act = (_silu(hg) * hu * w).astype(BF16)
            yc = yc + jnp.dot(act, wd_ref[e], preferred_element_type=F32)
        slot_c = lax.broadcasted_iota(jnp.int32, (tm, cap), 1).astype(F32) + base
        back = ((rank_col == slot_c) & (mem_col > 0.0)).astype(BF16)
        acc_ref[...] += jnp.dot(back, yc.astype(BF16), preferred_element_type=F32)
        return carry

    lax.fori_loop(0, (count + cap - 1) // cap, chunk, 0)

    @pl.when(g == pl.num_programs(1) - 1)
    def _():
        h3 = h2_ref[...] + acc_ref[...]
        if final_norm:
            h3 = h3 * lax.rsqrt(jnp.mean(h3 * h3, axis=-1, keepdims=True) + RMS_EPS) * gf_ref[...]
        o_ref[...] = h3


def _moe(hm, comb, h2, w_gate, w_up, w_down, norm_final, final_norm, tm, cap):
    t, d_model = hm.shape
    n_exp, _, d_exp = w_gate.shape
    n_local = EXPERTS_PER_GROUP
    n_groups = n_exp // n_local
    tm = min(tm, t)
    cap = min(cap, tm)
    idx = jnp.arange(tm)
    upper = (idx[:, None] < idx[None, :]).astype(BF16)
    lanes = jnp.arange(LANES)
    gmap = ((lanes[:, None] // n_local == lanes[None, :]) & (lanes[:, None] < n_exp)).astype(BF16)
    row = lambda width, **kw: pl.BlockSpec((tm, width), lambda i, g: (i, 0), **kw)
    once = dict(pipeline_mode=pl.Buffered(1))
    return pl.pallas_call(
        functools.partial(_moe_kernel, final_norm=final_norm, cap=cap),
        grid=(t // tm, n_groups),
        in_specs=[row(d_model), row(LANES), row(d_model, **once),
                  pl.BlockSpec((tm, tm), lambda i, g: (0, 0), **once),
                  pl.BlockSpec((LANES, LANES), lambda i, g: (0, 0), **once),
                  pl.BlockSpec((n_local, d_model, d_exp), lambda i, g: (g, 0, 0)),
                  pl.BlockSpec((n_local, d_model, d_exp), lambda i, g: (g, 0, 0)),
                  pl.BlockSpec((n_local, d_exp, d_model), lambda i, g: (g, 0, 0)),
                  pl.BlockSpec((1, d_model), lambda i, g: (0, 0))],
        out_specs=row(d_model, **once),
        out_shape=jax.ShapeDtypeStruct((t, d_model), F32),
        scratch_shapes=[pltpu.VMEM((tm, d_model), F32),
                        pltpu.VMEM((tm, LANES), F32), pltpu.VMEM((tm, LANES), F32),
                        pltpu.VMEM((LANES, tm), F32), pltpu.VMEM((LANES, tm), F32)],
        compiler_params=_params(2),
        name="moe_experts",
    )(hm, comb, h2, upper, gmap, w_gate, w_up, w_down, norm_final.astype(F32).reshape(1, d_model))


def kernel(x, mem, positions, norm_mix, w_in, conv_w, conv_b, dt_bias, a_log, d_skip, ssd_norm, w_ssd_out, q_a_norm, w_q_b, kv_a_norm, w_kv_b, w_mla_out, w_o, norm_xattn, norm_mem, w_xq, w_xkv, w_xo, norm_moe, w_router_group, b_router_group, w_router_expert, b_router_expert, w_exp_gate, w_exp_up, w_exp_down, norm_final):
    b, s, d_model = x.shape
    t = b * s
    n_mem = mem.shape[1]
    depth = norm_mix.shape[0]
    d_inner = ssd_norm.shape[1]
    xbc_dim = conv_w.shape[2]
    heads = a_log.shape[1]
    q_rank, kv_rank = w_q_b.shape[1], w_kv_b.shape[1]
    n_groups = w_router_group.shape[2]
    n_exp = w_router_expert.shape[2]

    o_z, o_xbc = 0, d_inner
    o_dt = o_xbc + xbc_dim
    o_qa = o_dt + heads
    o_gs = o_qa + q_rank + kv_rank + QK_ROPE
    o_end = o_gs + 2 * d_model

    h = x.reshape(t, d_model).astype(F32)
    for i in range(depth):
        wi = w_in[i]
        w_z = wi[:, o_z:o_xbc].astype(BF16)
        w_xbc = wi[:, o_xbc:o_dt].astype(BF16)
        qkv_width = q_rank + kv_rank + LANES
        w_small = jnp.concatenate([jnp.pad(wi[:, o_qa:o_gs], ((0, 0), (0, LANES - QK_ROPE))),
                                   jnp.pad(wi[:, o_dt:o_qa], ((0, 0), (0, LANES - heads)))], axis=1).astype(BF16)
        w_gates = wi[:, o_gs:o_end].astype(BF16)

        zb = _norm_matmul(h, norm_mix[i], w_z, BF16, 1024, 1024).reshape(b, s, d_inner)
        xbc = _norm_matmul(h, norm_mix[i], w_xbc, BF16, 1024, 1024).reshape(b, s, xbc_dim)
        gates = _norm_matmul(h, norm_mix[i], w_gates, BF16, 1024, 1024)
        small = _norm_matmul(h, norm_mix[i], w_small, F32, 1024, w_small.shape[1])

        yn = _ssd(xbc, zb, small.reshape(b, s, w_small.shape[1]), qkv_width // LANES, conv_w[i], conv_b[i],
                  dt_bias[i], a_log[i], d_skip[i], ssd_norm[i])

        hd = MLA_HEADS * LANES
        q, k, v = _mla_prep(small, qkv_width, positions, q_a_norm[i], w_q_b[i], kv_a_norm[i], w_kv_b[i], 512)
        o = _flash(q.reshape(b, s, hd), k.reshape(b, s, hd), v.reshape(b, s, hd), FLASH_TQ, FLASH_TK, FLASH_HP)

        w_mla_pad = jnp.pad(w_mla_out[i].reshape(MLA_HEADS, V_HEAD, d_model),
                            ((0, 0), (0, LANES - V_HEAD), (0, 0))).reshape(hd, d_model).astype(BF16)
        h1 = _merge(yn.reshape(t, d_inner), o.reshape(t, hd), gates, h, w_ssd_out[i].astype(BF16), w_mla_pad,
                    w_o[i].astype(BF16), 256)

        kvx = _norm_matmul(mem.reshape(b * n_mem, d_model).astype(F32), norm_mem[i], w_xkv[i].astype(BF16), BF16,
                           b * n_mem, 1024).reshape(b, n_mem, 2 * d_model)
        w_r = jnp.concatenate([jnp.pad(w_router_group[i].astype(F32), ((0, 0), (0, LANES - n_groups))),
                               jnp.pad(w_router_expert[i].astype(F32), ((0, 0), (0, LANES - n_exp)))], axis=1)
        wr_hi = w_r.astype(BF16)
        wr_lo = (w_r - wr_hi.astype(F32)).astype(BF16)
        b_r = jnp.concatenate([jnp.pad(b_router_group[i].astype(F32), (0, LANES - n_groups), constant_values=NEG),
                               jnp.pad(b_router_expert[i].astype(F32), (0, LANES - n_exp), constant_values=NEG)]
                              ).reshape(1, 2 * LANES)
        h2, hm, comb = _xattn_router(h1, kvx, s, norm_xattn[i], norm_moe[i], w_xq[i].astype(BF16),
                                     w_xo[i].astype(BF16), wr_hi, wr_lo, b_r, 512)

        h = _moe(hm, comb, h2, w_exp_gate[i].astype(BF16), w_exp_up[i].astype(BF16), w_exp_down[i].astype(BF16),
                 norm_final, i == depth - 1, MOE_TM, MOE_CAP)
    return h.reshape(b, s, d_model)
```

```python
import functools

import jax
import jax.numpy as jnp
from jax import lax
from jax.experimental import pallas as pl
from jax.experimental.pallas import tpu as pltpu

F32 = jnp.float32
BF16 = jnp.bfloat16
RMS_EPS = 1e-6
NEG = -1e30
ROPE_THETA = 10000.0
LOG2_E = 1.4426950408889634

LANES = 128
SSD_CHUNK = 128
SSD_SUB = 2
SSD_HEAD_DIM = 64
SSD_GROUPS = 4
SSD_STATE = 128
CONV_WIDTH = 4
MLA_HEADS = 16
QK_NOPE = 64
QK_ROPE = 32
V_HEAD = 64
XA_HEADS = 4
EXPERTS_PER_GROUP = 8
VMEM_LIMIT = 56 * 1024 * 1024
FLASH_TQ = 1024
FLASH_TK = 512
FLASH_HP = 2
MOE_TM = 1024
MOE_CAP = 320


def _params(n_axes):
    return pltpu.CompilerParams(dimension_semantics=("arbitrary",) * n_axes,
                                vmem_limit_bytes=VMEM_LIMIT)


def _sigmoid(v):
    return 1.0 / (1.0 + jnp.exp(-v))


def _silu(v):
    return v * _sigmoid(v)


def _split_bf16(v):
    hi = v.astype(BF16)
    lo = (v - hi.astype(F32)).astype(BF16)
    return hi, lo


def _const_spec(shape):
    return pl.BlockSpec(shape, lambda *_: (0,) * len(shape))


def _norm_matmul_kernel(x_ref, g_ref, w_ref, o_ref, un_ref):
    @pl.when(pl.program_id(1) == 0)
    def _():
        x = x_ref[...]
        r = lax.rsqrt(jnp.mean(x * x, axis=-1, keepdims=True) + RMS_EPS)
        un_ref[...] = (x * r * g_ref[...]).astype(BF16)

    o_ref[...] = jnp.dot(un_ref[...], w_ref[...], preferred_element_type=F32).astype(o_ref.dtype)


def _norm_matmul(x, gain, w, out_dtype, tm, tn):
    m, k = x.shape
    n = w.shape[1]
    tm, tn = min(tm, m), min(tn, n)
    return pl.pallas_call(
        _norm_matmul_kernel,
        grid=(m // tm, n // tn),
        in_specs=[pl.BlockSpec((tm, k), lambda i, j: (i, 0)),
                  pl.BlockSpec((1, k), lambda i, j: (0, 0)),
                  pl.BlockSpec((k, tn), lambda i, j: (0, j))],
        out_specs=pl.BlockSpec((tm, tn), lambda i, j: (i, j)),
        out_shape=jax.ShapeDtypeStruct((m, n), out_dtype),
        scratch_shapes=[pltpu.VMEM((tm, k), BF16)],
        compiler_params=_params(2),
        name="norm_matmul",
    )(x, gain.reshape(1, k), w)


def _ssd_kernel(xbc_ref, z_ref, dtr_ref, cw_ref, cb_ref, shift_ref, dtb_ref, alog_ref, expand_ref, dskip_ref,
                gn_ref, o_ref, cbuf_ref, state_ref, *, d_inner, heads):
    l = SSD_CHUNK
    n_sub = xbc_ref.shape[1] // l

    @pl.when(pl.program_id(1) == 0)
    def _():
        cbuf_ref[0:l, :] = jnp.zeros((l, cbuf_ref.shape[1]), cbuf_ref.dtype)
        state_ref[...] = jnp.zeros(state_ref.shape, F32)

    cbuf_ref[l:, :] = xbc_ref[0]
    for sub in range(n_sub):
        _ssd_chunk(sub, xbc_ref, z_ref, dtr_ref, cw_ref, cb_ref, shift_ref, dtb_ref, alog_ref, expand_ref,
                   dskip_ref, gn_ref, o_ref, cbuf_ref, state_ref, d_inner=d_inner, heads=heads)
    cbuf_ref[0:l, :] = xbc_ref[0, (n_sub - 1) * l:, :]


def _ssd_chunk(sub, xbc_ref, z_ref, dtr_ref, cw_ref, cb_ref, shift_ref, dtb_ref, alog_ref, expand_ref, dskip_ref,
               gn_ref, o_ref, cbuf_ref, state_ref, *, d_inner, heads):
    l = SSD_CHUNK
    gn = SSD_GROUPS * SSD_STATE
    gw = d_inner // SSD_GROUPS
    heads_per_group = heads // SSD_GROUPS
    rows = slice(sub * l, (sub + 1) * l)

    cur = xbc_ref[0, rows, :]
    shifted = jnp.dot(shift_ref[...], cbuf_ref[sub * l:(sub + 2) * l, :],
                      preferred_element_type=F32)
    conv = cb_ref[...] + cw_ref[CONV_WIDTH - 1:CONV_WIDTH, :] * cur.astype(F32)
    for lag in range(1, CONV_WIDTH):
        k = CONV_WIDTH - 1 - lag
        conv = conv + cw_ref[k:k + 1, :] * shifted[(lag - 1) * l:lag * l, :]
    conv = _silu(conv)
    xs = conv[:, :d_inner]
    bm = conv[:, d_inner:d_inner + gn]
    cm = conv[:, d_inner + gn:]

    dtv = dtr_ref[0, rows, :] + dtb_ref[...]
    dt = jnp.maximum(dtv, 0.0) + jnp.log1p(jnp.exp(-jnp.abs(dtv)))
    a_dt = dt * (-jnp.exp(alog_ref[...]))
    row_i = lax.broadcasted_iota(jnp.int32, (l, l), 0)
    col_i = lax.broadcasted_iota(jnp.int32, (l, l), 1)
    causal = col_i <= row_i
    tri = causal.astype(BF16)
    hi, lo = _split_bf16(a_dt)
    a_cum = jnp.dot(tri, hi, preferred_element_type=F32) + jnp.dot(tri, lo, preferred_element_type=F32)
    a_cum_t = a_cum.T

    ex = expand_ref[...]

    def expand(v):
        vh, vl = _split_bf16(v)
        return jnp.dot(vh, ex, preferred_element_type=F32) + jnp.dot(vl, ex, preferred_element_type=F32)

    dt_x = expand(dt)
    a_cum_x = expand(a_cum)
    last = a_cum_x[l - 1:l, :]
    grow = jnp.exp(a_cum_x)
    to_end = jnp.exp(last - a_cum_x)
    chunk_decay = jnp.exp(last)

    x_dt = xs * dt_x
    x_dt_b = x_dt.astype(BF16)
    x_end_b = (x_dt * to_end).astype(BF16)

    lane = lax.broadcasted_iota(jnp.int32, (l, LANES), 1)
    first_head = lane < SSD_HEAD_DIM
    outs = []
    for g in range(SSD_GROUPS):
        bg = bm[:, g * SSD_STATE:(g + 1) * SSD_STATE]
        cg = cm[:, g * SSD_STATE:(g + 1) * SSD_STATE].astype(BF16)
        cbm = lax.dot_general(cg, bg.astype(BF16), (((1,), (1,)), ((), ())), preferred_element_type=F32)
        cols = slice(g * gw, (g + 1) * gw)
        diag = []
        for pair in range(heads_per_group // 2):
            h0 = g * heads_per_group + 2 * pair
            xp = x_dt_b[:, h0 * SSD_HEAD_DIM:(h0 + 2) * SSD_HEAD_DIM]
            ys = []
            for h in (h0, h0 + 1):
                seg = a_cum[:, h:h + 1] - a_cum_t[h:h + 1, :]
                mh = (cbm * jnp.exp(jnp.where(causal, seg, NEG))).astype(BF16)
                ys.append(jnp.dot(mh, xp, preferred_element_type=F32))
            diag.append(jnp.where(first_head, ys[0], ys[1]))
        st = state_ref[g]
        y_off = jnp.dot(cg, st.astype(BF16), preferred_element_type=F32) * grow[:, cols]
        state_ref[g] = st * chunk_decay[:, cols] + jnp.dot(bg.T.astype(BF16), x_end_b[:, cols],
                                                          preferred_element_type=F32)
        outs.append(jnp.concatenate(diag, axis=1) + y_off)
    y = jnp.concatenate(outs, axis=1) + dskip_ref[...] * xs

    zz = z_ref[0, rows, :].astype(F32)
    y = y * _silu(zz)
    normed = []
    for g in range(SSD_GROUPS):
        yg = y[:, g * gw:(g + 1) * gw]
        normed.append(yg * lax.rsqrt(jnp.mean(yg * yg, axis=-1, keepdims=True) + RMS_EPS))
    o_ref[0, rows, :] = (jnp.concatenate(normed, axis=1) * gn_ref[...]).astype(o_ref.dtype)


def _ssd(xbc, z, dtr, dt_tile, conv_w, conv_b, dt_bias, a_log, d_skip, ssd_norm):
    b, s, xdim = xbc.shape
    d_inner = z.shape[2]
    heads = a_log.shape[0]
    l = SSD_CHUNK
    pad = LANES - heads
    dtb = jnp.pad(dt_bias.astype(F32), (0, pad)).reshape(1, LANES)
    alog = jnp.pad(a_log.astype(F32), (0, pad)).reshape(1, LANES)
    expand = (jnp.arange(LANES)[:, None] == (jnp.arange(d_inner)[None, :] // SSD_HEAD_DIM)).astype(BF16)
    dskip = jnp.repeat(d_skip.astype(F32), SSD_HEAD_DIM).reshape(1, d_inner)
    step = jnp.arange(l)
    shift = jnp.concatenate([(jnp.arange(2 * l)[None, :] == (l + step - lag)[:, None]).astype(BF16)
                             for lag in range(1, CONV_WIDTH)], axis=0)
    kern = functools.partial(_ssd_kernel, d_inner=d_inner, heads=heads)
    rows = SSD_SUB * l
    return pl.pallas_call(
        kern,
        grid=(b, s // rows),
        in_specs=[pl.BlockSpec((1, rows, xdim), lambda i, c: (i, c, 0)),
                  pl.BlockSpec((1, rows, d_inner), lambda i, c: (i, c, 0)),
                  pl.BlockSpec((1, rows, LANES), lambda i, c: (i, c, dt_tile)),
                  _const_spec((CONV_WIDTH, xdim)),
                  _const_spec((1, xdim)),
                  _const_spec(((CONV_WIDTH - 1) * l, 2 * l)),
                  _const_spec((1, LANES)),
                  _const_spec((1, LANES)),
                  _const_spec((LANES, d_inner)),
                  _const_spec((1, d_inner)),
                  _const_spec((1, d_inner))],
        out_specs=pl.BlockSpec((1, rows, d_inner), lambda i, c: (i, c, 0)),
        out_shape=jax.ShapeDtypeStruct((b, s, d_inner), BF16),
        scratch_shapes=[pltpu.VMEM((l + rows, xdim), BF16),
                        pltpu.VMEM((SSD_GROUPS, SSD_STATE, d_inner // SSD_GROUPS), F32)],
        compiler_params=_params(2),
        name="ssd_scan",
    )(xbc, z, dtr, conv_w.astype(F32), conv_b.astype(F32).reshape(1, xdim), shift, dtb, alog, expand, dskip,
      ssd_norm.astype(F32).reshape(1, d_inner))


def _mla_prep_kernel(a_ref, pos_ref, invf_ref, qg_ref, kg_ref, wq1_ref, wq2_ref, wk1_ref, wk2_ref, wv_ref,
                     q_ref, k_ref, v_ref, *, q_rank, kv_rank, scale):
    a = a_ref[...]
    qa = a[:, :q_rank]
    ckv = a[:, q_rank:q_rank + kv_rank]
    rest = a[:, q_rank + kv_rank:]
    qn = (qa * lax.rsqrt(jnp.mean(qa * qa, axis=-1, keepdims=True) + RMS_EPS) * qg_ref[...]).astype(BF16)
    cn = (ckv * lax.rsqrt(jnp.mean(ckv * ckv, axis=-1, keepdims=True) + RMS_EPS) * kg_ref[...]).astype(BF16)
    rest_b = rest.astype(BF16)

    ang = pos_ref[...].astype(F32) * invf_ref[...]
    cos = jnp.concatenate([jnp.cos(ang)] * MLA_HEADS, axis=1)
    sin = jnp.concatenate([jnp.sin(ang)] * MLA_HEADS, axis=1)

    q = jnp.dot(qn, wq1_ref[...], preferred_element_type=F32) * cos
    q = q + jnp.dot(qn, wq2_ref[...], preferred_element_type=F32) * sin
    q_ref[...] = (q * scale).astype(q_ref.dtype)

    kin = jnp.concatenate([cn, rest_b], axis=1)
    k = jnp.dot(kin, wk1_ref[...], preferred_element_type=F32) * cos
    k = k + jnp.dot(rest_b, wk2_ref[...], preferred_element_type=F32) * sin
    k_ref[...] = k.astype(k_ref.dtype)

    v = jnp.dot(cn, wv_ref[...], preferred_element_type=F32)
    lane = lax.broadcasted_iota(jnp.int32, v.shape, 1) % LANES
    v_ref[...] = jnp.where(lane == V_HEAD, 1.0, v).astype(v_ref.dtype)


def _mla_prep(qkv_a, width, positions, q_a_norm, w_q_b, kv_a_norm, w_kv_b, tm):
    t = qkv_a.shape[0]
    q_rank, kv_rank = w_q_b.shape[0], w_kv_b.shape[0]
    hd = MLA_HEADS * LANES
    qk_head = QK_NOPE + QK_ROPE
    half = QK_ROPE // 2
    tm = min(tm, t)

    w3 = w_q_b.astype(F32).reshape(q_rank, MLA_HEADS, qk_head)
    wq1 = jnp.pad(w3, ((0, 0), (0, 0), (0, LANES - qk_head))).reshape(q_rank, hd).astype(BF16)
    pe = w3[..., QK_NOPE:]
    rot = jnp.concatenate([-pe[..., half:], pe[..., :half]], axis=-1)
    wq2 = jnp.pad(rot, ((0, 0), (0, 0), (QK_NOPE, LANES - qk_head))).reshape(q_rank, hd).astype(BF16)

    kv3 = w_kv_b.astype(F32).reshape(kv_rank, MLA_HEADS, QK_NOPE + V_HEAD)
    wk_nope = jnp.pad(kv3[..., :QK_NOPE], ((0, 0), (0, 0), (0, LANES - QK_NOPE))).reshape(kv_rank, hd)
    wv = jnp.pad(kv3[..., QK_NOPE:], ((0, 0), (0, 0), (0, LANES - V_HEAD))).reshape(kv_rank, hd).astype(BF16)
    eye = jnp.eye(QK_ROPE, dtype=F32)
    place = jnp.pad(eye, ((0, LANES - QK_ROPE), (QK_NOPE, LANES - qk_head)))
    eye_rot = jnp.concatenate([-eye[:, half:], eye[:, :half]], axis=1)
    place_rot = jnp.pad(eye_rot, ((0, LANES - QK_ROPE), (QK_NOPE, LANES - qk_head)))
    wk1 = jnp.concatenate([wk_nope, jnp.tile(place, (1, MLA_HEADS))], axis=0).astype(BF16)
    wk2 = jnp.tile(place_rot, (1, MLA_HEADS)).astype(BF16)

    inv_freq = ROPE_THETA ** (-jnp.arange(half, dtype=F32) / half)
    invf = jnp.concatenate([jnp.zeros((QK_NOPE,), F32), inv_freq, inv_freq,
                            jnp.zeros((LANES - qk_head,), F32)]).reshape(1, LANES)

    kern = functools.partial(_mla_prep_kernel, q_rank=q_rank, kv_rank=kv_rank,
                             scale=qk_head ** -0.5 * LOG2_E)
    out = jax.ShapeDtypeStruct((t, hd), BF16)
    return pl.pallas_call(
        kern,
        grid=(t // tm,),
        in_specs=[pl.BlockSpec((tm, width), lambda i: (i, 0)),
                  pl.BlockSpec((tm, 1), lambda i: (i, 0)),
                  _const_spec((1, LANES)),
                  _const_spec((1, q_rank)),
                  _const_spec((1, kv_rank)),
                  _const_spec((q_rank, hd)),
                  _const_spec((q_rank, hd)),
                  _const_spec((kv_rank + LANES, hd)),
                  _const_spec((LANES, hd)),
                  _const_spec((kv_rank, hd))],
        out_specs=[pl.BlockSpec((tm, hd), lambda i: (i, 0))] * 3,
        out_shape=[out, out, out],
        compiler_params=_params(1),
        name="mla_prep",
    )(qkv_a, positions.reshape(t, 1), invf, q_a_norm.astype(F32).reshape(1, q_rank),
      kv_a_norm.astype(F32).reshape(1, kv_rank), wq1, wq2, wk1, wk2, wv)


def _flash_kernel(q_ref, k_ref, v_ref, o_ref, s_ref, m_ref, acc_ref, *, tq, tk, hp):
    i = pl.program_id(2)
    diag_blocks = tq // tk
    assert diag_blocks == 2
    n_tiles = tk // LANES
    m_ref[...] = jnp.full(m_ref.shape, NEG, F32)
    acc_ref[...] = jnp.zeros(acc_ref.shape, F32)

    def rows(j):
        return pl.ds(pl.multiple_of(j * tk, tk), tk)

    def head(h):
        return slice(h * LANES, (h + 1) * LANES)

    def scores(j, slot, row0=0):
        for h in range(hp):
            s_ref[2 * h + slot, row0:, :] = lax.dot_general(
                q_ref[0, row0:, head(h)], k_ref[0, rows(j), head(h)],
                (((1,), (1,)), ((), ())), preferred_element_type=F32)

    def update(j, slot, key_offset=None, row0=0):
        nr = tq - row0
        for h in range(hp):
            tiles = [s_ref[2 * h + slot, row0:, c * LANES:(c + 1) * LANES] for c in range(n_tiles)]
            if key_offset is not None:
                r = lax.broadcasted_iota(jnp.int32, (nr, LANES), 0) + row0
                c0 = lax.broadcasted_iota(jnp.int32, (nr, LANES), 1)
                tiles = [jnp.where(c0 + (key_offset + c * LANES) <= r, t, NEG) for c, t in enumerate(tiles)]
            m_old = m_ref[h, row0:, :]
            m_new = jnp.maximum(m_old, jnp.max(functools.reduce(jnp.maximum, tiles), axis=-1, keepdims=True))
            p = jnp.concatenate([jnp.exp2(t - m_new) for t in tiles], axis=1).astype(BF16)
            acc_ref[h, row0:, :] = (jnp.exp2(m_old - m_new) * acc_ref[h, row0:, :]
                                    + jnp.dot(p, v_ref[0, rows(j), head(h)], preferred_element_type=F32))
            m_ref[h, row0:, :] = m_new

    scores(0, 0)

    def body(jj, carry):
        j = 2 * jj
        scores(j + 1, 1)
        update(j, 0)
        scores(j + 2, 0)
        update(j + 1, 1)
        return carry

    lax.fori_loop(0, i, body, 0)
    scores(2 * i + 1, 1, row0=tk)
    update(2 * i, 0, key_offset=0)
    update(2 * i + 1, 1, key_offset=tk, row0=tk)
    for h in range(hp):
        acc = acc_ref[h]
        o_ref[0, :, head(h)] = (acc / acc[:, V_HEAD:V_HEAD + 1]).astype(o_ref.dtype)


def _flash(q, k, v, tq, tk, hp):
    b, s, hd = q.shape
    w = hp * LANES
    return pl.pallas_call(
        functools.partial(_flash_kernel, tq=tq, tk=tk, hp=hp),
        grid=(b, hd // w, s // tq),
        in_specs=[pl.BlockSpec((1, tq, w), lambda bi, h, i: (bi, i, h)),
                  pl.BlockSpec((1, s, w), lambda bi, h, i: (bi, 0, h)),
                  pl.BlockSpec((1, s, w), lambda bi, h, i: (bi, 0, h))],
        out_specs=pl.BlockSpec((1, tq, w), lambda bi, h, i: (bi, i, h)),
        out_shape=jax.ShapeDtypeStruct((b, s, hd), BF16),
        scratch_shapes=[pltpu.VMEM((2 * hp, tq, tk), F32),
                        pltpu.VMEM((hp, tq, LANES), F32), pltpu.VMEM((hp, tq, LANES), F32)],
        compiler_params=_params(3),
        name="mla_flash",
    )(q, k, v)


def _merge_kernel(yn_ref, o_ref, g_ref, x_ref, wso_ref, wmo_ref, wo_ref, h_ref, *, d_model):
    y_ssd = jnp.dot(yn_ref[...], wso_ref[...], preferred_element_type=F32)
    y_mla = jnp.dot(o_ref[...], wmo_ref[...], preferred_element_type=F32)
    g = g_ref[...].astype(F32)
    merged = _sigmoid(g[:, :d_model]) * y_ssd + _sigmoid(g[:, d_model:]) * y_mla
    h_ref[...] = x_ref[...] + jnp.dot(merged.astype(BF16), wo_ref[...], preferred_element_type=F32)


def _merge(yn, o, gates, x, w_ssd_out, w_mla_pad, w_o, tm):
    t, d_model = x.shape
    tm = min(tm, t)
    row = lambda width: pl.BlockSpec((tm, width), lambda i: (i, 0))
    return pl.pallas_call(
        functools.partial(_merge_kernel, d_model=d_model),
        grid=(t // tm,),
        in_specs=[row(yn.shape[1]), row(o.shape[1]), row(gates.shape[1]), row(d_model),
                  _const_spec(w_ssd_out.shape), _const_spec(w_mla_pad.shape), _const_spec(w_o.shape)],
        out_specs=row(d_model),
        out_shape=jax.ShapeDtypeStruct((t, d_model), F32),
        compiler_params=_params(1),
        name="merge",
    )(yn, o, gates, x, w_ssd_out, w_mla_pad, w_o)


def _xattn_router_kernel(h_ref, kv_ref, gx_ref, gm_ref, wq_ref, wo_ref, wrh_ref, wrl_ref, br_ref,
                         h2_ref, hm_ref, comb_ref, *, d_model):
    h1 = h_ref[...]
    hn = (h1 * lax.rsqrt(jnp.mean(h1 * h1, axis=-1, keepdims=True) + RMS_EPS) * gx_ref[...]).astype(BF16)
    q = jnp.dot(hn, wq_ref[...], preferred_element_type=F32).astype(BF16)
    hd = d_model // XA_HEADS
    kv = kv_ref[0]
    outs = []
    for hh in range(XA_HEADS):
        qh = q[:, hh * hd:(hh + 1) * hd]
        kh = kv[:, hh * hd:(hh + 1) * hd]
        vh = kv[:, d_model + hh * hd:d_model + (hh + 1) * hd]
        s = lax.dot_general(qh, kh, (((1,), (1,)), ((), ())), preferred_element_type=F32) * (hd ** -0.5)
        p = jnp.exp(s - jnp.max(s, axis=-1, keepdims=True))
        p = p / jnp.sum(p, axis=-1, keepdims=True)
        outs.append(jnp.dot(p.astype(BF16), vh, preferred_element_type=F32))
    ox = jnp.concatenate(outs, axis=1).astype(BF16)
    h2 = h1 + jnp.dot(ox, wo_ref[...], preferred_element_type=F32)
    h2_ref[...] = h2

    hm = h2 * lax.rsqrt(jnp.mean(h2 * h2, axis=-1, keepdims=True) + RMS_EPS) * gm_ref[...]
    hm_ref[...] = hm.astype(hm_ref.dtype)

    hi, lo = _split_bf16(hm)
    logits = (jnp.dot(hi, wrh_ref[...], preferred_element_type=F32)
              + jnp.dot(lo, wrh_ref[...], preferred_element_type=F32)
              + jnp.dot(hi, wrl_ref[...], preferred_element_type=F32)) + br_ref[...]
    gl = logits[:, :LANES]
    el = logits[:, LANES:]
    lane = lax.broadcasted_iota(jnp.int32, gl.shape, 1)

    def first_argmax(v, vmax):
        return jnp.min(jnp.where(v == vmax, lane, LANES), axis=-1, keepdims=True)

    gmax = jnp.max(gl, axis=-1, keepdims=True)
    g_sel = first_argmax(gl, gmax)
    g_w = 1.0 / jnp.sum(jnp.exp(gl - gmax), axis=-1, keepdims=True)
    lo_lane = g_sel * EXPERTS_PER_GROUP
    in_group = (lane >= lo_lane) & (lane < lo_lane + EXPERTS_PER_GROUP)
    e1 = jnp.where(in_group, el, NEG)
    v1 = jnp.max(e1, axis=-1, keepdims=True)
    i1 = first_argmax(e1, v1)
    e2 = jnp.where(lane == i1, NEG, e1)
    v2 = jnp.max(e2, axis=-1, keepdims=True)
    i2 = first_argmax(e2, v2)
    r = jnp.exp(v2 - v1)
    w1 = g_w / (1.0 + r)
    w2 = g_w * r / (1.0 + r)
    comb_ref[...] = jnp.where(lane == i1, w1, 0.0) + jnp.where(lane == i2, w2, 0.0)


def _xattn_router(h1, kvx, tokens_per_batch, norm_xattn, norm_moe, w_xq, w_xo, wr_hi, wr_lo, b_r, tm):
    t, d_model = h1.shape
    tm = min(tm, tokens_per_batch)
    per_b = tokens_per_batch // tm
    n_mem = kvx.shape[1]
    row = lambda width: pl.BlockSpec((tm, width), lambda i: (i, 0))
    return pl.pallas_call(
        functools.partial(_xattn_router_kernel, d_model=d_model),
        grid=(t // tm,),
        in_specs=[row(d_model),
                  pl.BlockSpec((1, n_mem, 2 * d_model), lambda i: (i // per_b, 0, 0)),
                  _const_spec((1, d_model)), _const_spec((1, d_model)),
                  _const_spec(w_xq.shape), _const_spec(w_xo.shape),
                  _const_spec(wr_hi.shape), _const_spec(wr_lo.shape), _const_spec(b_r.shape)],
        out_specs=[row(d_model), row(d_model), row(LANES)],
        out_shape=[jax.ShapeDtypeStruct((t, d_model), F32),
                   jax.ShapeDtypeStruct((t, d_model), BF16),
                   jax.ShapeDtypeStruct((t, LANES), F32)],
        compiler_params=_params(1),
        name="xattn_router",
    )(h1, kvx, norm_xattn.astype(F32).reshape(1, d_model), norm_moe.astype(F32).reshape(1, d_model),
      w_xq, w_xo, wr_hi, wr_lo, b_r)


def _moe_kernel(hm_ref, comb_ref, h2_ref, upper_ref, gmap_ref, wg_ref, wu_ref, wd_ref, gf_ref, o_ref,
                acc_ref, mem_ref, rank_ref, mem_t_ref, rank_t_ref, *, final_norm, cap):
    g = pl.program_id(1)
    tm = hm_ref.shape[0]
    n_local = wg_ref.shape[0]
    comb = comb_ref[...]
    lane = lax.broadcasted_iota(jnp.int32, (tm, LANES), 1)

    @pl.when(g == 0)
    def _():
        acc_ref[...] = jnp.zeros(acc_ref.shape, F32)
        chosen = (comb > 0.0).astype(BF16)
        member = (jnp.dot(chosen, gmap_ref[...], preferred_element_type=F32) > 0.0).astype(F32)
        member_t = member.T
        rank_t = jnp.dot(member_t.astype(BF16), upper_ref[...], preferred_element_type=F32)
        mem_ref[...] = member
        mem_t_ref[...] = member_t
        rank_t_ref[...] = rank_t
        rank_ref[...] = rank_t.T

    mem_row = mem_t_ref[pl.ds(g, 1), :]
    rank_row = rank_t_ref[pl.ds(g, 1), :]
    pick = lane == g
    mem_col = jnp.sum(jnp.where(pick, mem_ref[...], 0.0), axis=-1, keepdims=True)
    rank_col = jnp.sum(jnp.where(pick, rank_ref[...], 0.0), axis=-1, keepdims=True)
    count = jnp.sum(mem_row).astype(jnp.int32)
    comb_hi, comb_lo = _split_bf16(comb)
    hm = hm_ref[...]

    def chunk(c, carry):
        base = (c * cap).astype(F32)
        slot_r = lax.broadcasted_iota(jnp.int32, (cap, tm), 0).astype(F32) + base
        sel = ((rank_row == slot_r) & (mem_row > 0.0)).astype(BF16)
        xc = jnp.dot(sel, hm, preferred_element_type=F32).astype(BF16)
        wc = (jnp.dot(sel, comb_hi, preferred_element_type=F32)
              + jnp.dot(sel, comb_lo, preferred_element_type=F32))
        lane_c = lax.broadcasted_iota(jnp.int32, (cap, LANES), 1)
        yc = jnp.zeros((cap, hm.shape[1]), F32)
        for e in range(n_local):
            w = jnp.sum(jnp.where(lane_c == g * n_local + e, wc, 0.0), axis=-1, keepdims=True)
            hg = jnp.dot(xc, wg_ref[e], preferred_element_type=F32)
            hu = jnp.dot(xc, wu_ref[e], preferred_element_type=F32)
            act = (_silu(hg) * hu * w).astype(BF16)
            yc = yc + jnp.dot(act, wd_ref[e], preferred_element_type=F32)
        slot_c = lax.broadcasted_iota(jnp.int32, (tm, cap), 1).astype(F32) + base
        back = ((rank_col == slot_c) & (mem_col > 0.0)).astype(BF16)
        acc_ref[...] += jnp.dot(back, yc.astype(BF16), preferred_element_type=F32)
        return carry

    lax.fori_loop(0, (count + cap - 1) // cap, chunk, 0)

    @pl.when(g == pl.num_programs(1) - 1)
    def _():
        h3 = h2_ref[...] + acc_ref[...]
        if final_norm:
            h3 = h3 * lax.rsqrt(jnp.mean(h3 * h3, axis=-1, keepdims=True) + RMS_EPS) * gf_ref[...]
        o_ref[...] = h3


def _moe(hm, comb, h2, w_gate, w_up, w_down, norm_final, final_norm, tm, cap):
    t, d_model = hm.shape
    n_exp, _, d_exp = w_gate.shape
    n_local = EXPERTS_PER_GROUP
    n_groups = n_exp // n_local
    tm = min(tm, t)
    cap = min(cap, tm)
    idx = jnp.arange(tm)
    upper = (idx[:, None] < idx[None, :]).astype(BF16)
    lanes = jnp.arange(LANES)
    gmap = ((lanes[:, None] // n_local == lanes[None, :]) & (lanes[:, None] < n_exp)).astype(BF16)
    row = lambda width, **kw: pl.BlockSpec((tm, width), lambda i, g: (i, 0), **kw)
    once = dict(pipeline_mode=pl.Buffered(1))
    return pl.pallas_call(
        functools.partial(_moe_kernel, final_norm=final_norm, cap=cap),
        grid=(t // tm, n_groups),
        in_specs=[row(d_model), row(LANES), row(d_model, **once),
                  pl.BlockSpec((tm, tm), lambda i, g: (0, 0), **once),
                  pl.BlockSpec((LANES, LANES), lambda i, g: (0, 0), **once),
                  pl.BlockSpec((n_local, d_model, d_exp), lambda i, g: (g, 0, 0)),
                  pl.BlockSpec((n_local, d_model, d_exp), lambda i, g: (g, 0, 0)),
                  pl.BlockSpec((n_local, d_exp, d_model), lambda i, g: (g, 0, 0)),
                  pl.BlockSpec((1, d_model), lambda i, g: (0, 0))],
        out_specs=row(d_model, **once),
        out_shape=jax.ShapeDtypeStruct((t, d_model), F32),
        scratch_shapes=[pltpu.VMEM((tm, d_model), F32),
                        pltpu.VMEM((tm, LANES), F32), pltpu.VMEM((tm, LANES), F32),
                        pltpu.VMEM((LANES, tm), F32), pltpu.VMEM((LANES, tm), F32)],
        compiler_params=_params(2),
        name="moe_experts",
    )(hm, comb, h2, upper, gmap, w_gate, w_up, w_down, norm_final.astype(F32).reshape(1, d_model))


def kernel(x, mem, positions, norm_mix, w_in, conv_w, conv_b, dt_bias, a_log, d_skip, ssd_norm, w_ssd_out, q_a_norm, w_q_b, kv_a_norm, w_kv_b, w_mla_out, w_o, norm_xattn, norm_mem, w_xq, w_xkv, w_xo, norm_moe, w_router_group, b_router_group, w_router_expert, b_router_expert, w_exp_gate, w_exp_up, w_exp_down, norm_final):
    b, s, d_model = x.shape
    t = b * s
    n_mem = mem.shape[1]
    depth = norm_mix.shape[0]
    d_inner = ssd_norm.shape[1]
    xbc_dim = conv_w.shape[2]
    heads = a_log.shape[1]
    q_rank, kv_rank = w_q_b.shape[1], w_kv_b.shape[1]
    n_groups = w_router_group.shape[2]
    n_exp = w_router_expert.shape[2]

    o_z, o_xbc = 0, d_inner
    o_dt = o_xbc + xbc_dim
    o_qa = o_dt + heads
    o_gs = o_qa + q_rank + kv_rank + QK_ROPE
    o_end = o_gs + 2 * d_model

    h = x.reshape(t, d_model).astype(F32)
    for i in range(depth):
        wi = w_in[i]
        w_z = wi[:, o_z:o_xbc].astype(BF16)
        w_xbc = wi[:, o_xbc:o_dt].astype(BF16)
        qkv_width = q_rank + kv_rank + LANES
        w_small = jnp.concatenate([jnp.pad(wi[:, o_qa:o_gs], ((0, 0), (0, LANES - QK_ROPE))),
                                   jnp.pad(wi[:, o_dt:o_qa], ((0, 0), (0, LANES - heads)))], axis=1).astype(BF16)
        w_gates = wi[:, o_gs:o_end].astype(BF16)

        zb = _norm_matmul(h, norm_mix[i], w_z, BF16, 1024, d_inner).reshape(b, s, d_inner)
        xbc = _norm_matmul(h, norm_mix[i], w_xbc, BF16, 1024, xbc_dim).reshape(b, s, xbc_dim)
        gates = _norm_matmul(h, norm_mix[i], w_gates, BF16, 1024, 2 * d_model)
        small = _norm_matmul(h, norm_mix[i], w_small, F32, 1024, w_small.shape[1])

        yn = _ssd(xbc, zb, small.reshape(b, s, w_small.shape[1]), qkv_width // LANES, conv_w[i], conv_b[i],
                  dt_bias[i], a_log[i], d_skip[i], ssd_norm[i])

        hd = MLA_HEADS * LANES
        q, k, v = _mla_prep(small, qkv_width, positions, q_a_norm[i], w_q_b[i], kv_a_norm[i], w_kv_b[i], 512)
        o = _flash(q.reshape(b, s, hd), k.reshape(b, s, hd), v.reshape(b, s, hd), FLASH_TQ, FLASH_TK, FLASH_HP)

        w_mla_pad = jnp.pad(w_mla_out[i].reshape(MLA_HEADS, V_HEAD, d_model),
                            ((0, 0), (0, LANES - V_HEAD), (0, 0))).reshape(hd, d_model).astype(BF16)
        h1 = _merge(yn.reshape(t, d_inner), o.reshape(t, hd), gates, h, w_ssd_out[i].astype(BF16), w_mla_pad,
                    w_o[i].astype(BF16), 256)

        kvx = _norm_matmul(mem.reshape(b * n_mem, d_model).astype(F32), norm_mem[i], w_xkv[i].astype(BF16), BF16,
                           b * n_mem, 1024).reshape(b, n_mem, 2 * d_model)
        w_r = jnp.concatenate([jnp.pad(w_router_group[i].astype(F32), ((0, 0), (0, LANES - n_groups))),
                               jnp.pad(w_router_expert[i].astype(F32), ((0, 0), (0, LANES - n_exp)))], axis=1)
        wr_hi = w_r.astype(BF16)
        wr_lo = (w_r - wr_hi.astype(F32)).astype(BF16)
        b_r = jnp.concatenate([jnp.pad(b_router_group[i].astype(F32), (0, LANES - n_groups), constant_values=NEG),
                               jnp.pad(b_router_expert[i].astype(F32), (0, LANES - n_exp), constant_values=NEG)]
                              ).reshape(1, 2 * LANES)
        h2, hm, comb = _xattn_router(h1, kvx, s, norm_xattn[i], norm_moe[i], w_xq[i].astype(BF16),
                                     w_xo[i].astype(BF16), wr_hi, wr_lo, b_r, 512)

        h = _moe(hm, comb, h2, w_exp_gate[i].astype(BF16), w_exp_up[i].astype(BF16), w_exp_down[i].astype(BF16),
                 norm_final, i == depth - 1, MOE_TM, MOE_CAP)
    return h.reshape(b, s, d_model)
```

```python
import functools

import jax
import jax.numpy as jnp
from jax import lax
from jax.experimental import pallas as pl
from jax.experimental.pallas import tpu as pltpu

F32 = jnp.float32
BF16 = jnp.bfloat16
RMS_EPS = 1e-6
NEG = -1e30
ROPE_THETA = 10000.0
LOG2_E = 1.4426950408889634

LANES = 128
SSD_CHUNK = 128
SSD_SUB = 2
SSD_HEAD_DIM = 64
SSD_GROUPS = 4
SSD_STATE = 128
CONV_WIDTH = 4
MLA_HEADS = 16
QK_NOPE = 64
QK_ROPE = 32
V_HEAD = 64
XA_HEADS = 4
EXPERTS_PER_GROUP = 8
VMEM_LIMIT = 56 * 1024 * 1024
FLASH_TQ = 1024
FLASH_TK = 512
FLASH_HP = 2
MOE_TM = 1024
MOE_CAP = 320


def _params(n_axes):
    return pltpu.CompilerParams(dimension_semantics=("arbitrary",) * n_axes,
                                vmem_limit_bytes=VMEM_LIMIT)


def _sigmoid(v):
    return 1.0 / (1.0 + jnp.exp(-v))


def _silu(v):
    return v * _sigmoid(v)


def _split_bf16(v):
    hi = v.astype(BF16)
    lo = (v - hi.astype(F32)).astype(BF16)
    return hi, lo


def _const_spec(shape):
    return pl.BlockSpec(shape, lambda *_: (0,) * len(shape))


def _norm_matmul_kernel(x_ref, g_ref, w_ref, o_ref, un_ref):
    @pl.when(pl.program_id(1) == 0)
    def _():
        x = x_ref[...]
        r = lax.rsqrt(jnp.mean(x * x, axis=-1, keepdims=True) + RMS_EPS)
        un_ref[...] = (x * r * g_ref[...]).astype(BF16)

    o_ref[...] = jnp.dot(un_ref[...], w_ref[...], preferred_element_type=F32).astype(o_ref.dtype)


def _norm_matmul(x, gain, w, out_dtype, tm, tn):
    m, k = x.shape
    n = w.shape[1]
    tm, tn = min(tm, m), min(tn, n)
    return pl.pallas_call(
        _norm_matmul_kernel,
        grid=(m // tm, n // tn),
        in_specs=[pl.BlockSpec((tm, k), lambda i, j: (i, 0)),
                  pl.BlockSpec((1, k), lambda i, j: (0, 0)),
                  pl.BlockSpec((k, tn), lambda i, j: (0, j))],
        out_specs=pl.BlockSpec((tm, tn), lambda i, j: (i, j)),
        out_shape=jax.ShapeDtypeStruct((m, n), out_dtype),
        scratch_shapes=[pltpu.VMEM((tm, k), BF16)],
        compiler_params=_params(2),
        name="norm_matmul",
    )(x, gain.reshape(1, k), w)


def _ssd_kernel(xbc_ref, z_ref, dtr_ref, cw_ref, cb_ref, shift_ref, dtb_ref, alog_ref, expand_ref, dskip_ref,
                gn_ref, o_ref, cbuf_ref, state_ref, *, d_inner, heads):
    l = SSD_CHUNK
    n_sub = xbc_ref.shape[1] // l

    @pl.when(pl.program_id(1) == 0)
    def _():
        cbuf_ref[0:l, :] = jnp.zeros((l, cbuf_ref.shape[1]), cbuf_ref.dtype)
        state_ref[...] = jnp.zeros(state_ref.shape, F32)

    cbuf_ref[l:, :] = xbc_ref[0]
    for sub in range(n_sub):
        _ssd_chunk(sub, xbc_ref, z_ref, dtr_ref, cw_ref, cb_ref, shift_ref, dtb_ref, alog_ref, expand_ref,
                   dskip_ref, gn_ref, o_ref, cbuf_ref, state_ref, d_inner=d_inner, heads=heads)
    cbuf_ref[0:l, :] = xbc_ref[0, (n_sub - 1) * l:, :]


def _ssd_chunk(sub, xbc_ref, z_ref, dtr_ref, cw_ref, cb_ref, shift_ref, dtb_ref, alog_ref, expand_ref, dskip_ref,
               gn_ref, o_ref, cbuf_ref, state_ref, *, d_inner, heads):
    l = SSD_CHUNK
    gn = SSD_GROUPS * SSD_STATE
    gw = d_inner // SSD_GROUPS
    heads_per_group = heads // SSD_GROUPS
    rows = slice(sub * l, (sub + 1) * l)

    cur = xbc_ref[0, rows, :]
    shifted = jnp.dot(shift_ref[...], cbuf_ref[sub * l:(sub + 2) * l, :],
                      preferred_element_type=F32)
    conv = cb_ref[...] + cw_ref[CONV_WIDTH - 1:CONV_WIDTH, :] * cur.astype(F32)
    for lag in range(1, CONV_WIDTH):
        k = CONV_WIDTH - 1 - lag
        conv = conv + cw_ref[k:k + 1, :] * shifted[(lag - 1) * l:lag * l, :]
    conv = _silu(conv)
    xs = conv[:, :d_inner]
    bm = conv[:, d_inner:d_inner + gn]
    cm = conv[:, d_inner + gn:]

    dtv = dtr_ref[0, rows, :] + dtb_ref[...]
    dt = jnp.maximum(dtv, 0.0) + jnp.log1p(jnp.exp(-jnp.abs(dtv)))
    a_dt = dt * (-jnp.exp(alog_ref[...]))
    row_i = lax.broadcasted_iota(jnp.int32, (l, l), 0)
    col_i = lax.broadcasted_iota(jnp.int32, (l, l), 1)
    causal = col_i <= row_i
    tri = causal.astype(BF16)
    hi, lo = _split_bf16(a_dt)
    a_cum = jnp.dot(tri, hi, preferred_element_type=F32) + jnp.dot(tri, lo, preferred_element_type=F32)
    a_cum_t = a_cum.T

    ex = expand_ref[...]

    def expand(v):
        vh, vl = _split_bf16(v)
        return jnp.dot(vh, ex, preferred_element_type=F32) + jnp.dot(vl, ex, preferred_element_type=F32)

    dt_x = expand(dt)
    a_cum_x = expand(a_cum)
    last = a_cum_x[l - 1:l, :]
    grow = jnp.exp(a_cum_x)
    to_end = jnp.exp(last - a_cum_x)
    chunk_decay = jnp.exp(last)

    x_dt = xs * dt_x
    x_dt_b = x_dt.astype(BF16)
    x_end_b = (x_dt * to_end).astype(BF16)

    lane = lax.broadcasted_iota(jnp.int32, (l, LANES), 1)
    first_head = lane < SSD_HEAD_DIM
    outs = []
    for g in range(SSD_GROUPS):
        bg = bm[:, g * SSD_STATE:(g + 1) * SSD_STATE]
        cg = cm[:, g * SSD_STATE:(g + 1) * SSD_STATE].astype(BF16)
        cbm = lax.dot_general(cg, bg.astype(BF16), (((1,), (1,)), ((), ())), preferred_element_type=F32)
        cols = slice(g * gw, (g + 1) * gw)
        diag = []
        for pair in range(heads_per_group // 2):
            h0 = g * heads_per_group + 2 * pair
            xp = x_dt_b[:, h0 * SSD_HEAD_DIM:(h0 + 2) * SSD_HEAD_DIM]
            ys = []
            for h in (h0, h0 + 1):
                seg = a_cum[:, h:h + 1] - a_cum_t[h:h + 1, :]
                mh = (cbm * jnp.exp(jnp.where(causal, seg, NEG))).astype(BF16)
                ys.append(jnp.dot(mh, xp, preferred_element_type=F32))
            diag.append(jnp.where(first_head, ys[0], ys[1]))
        st = state_ref[g]
        y_off = jnp.dot(cg, st.astype(BF16), preferred_element_type=F32) * grow[:, cols]
        state_ref[g] = st * chunk_decay[:, cols] + jnp.dot(bg.T.astype(BF16), x_end_b[:, cols],
                                                          preferred_element_type=F32)
        outs.append(jnp.concatenate(diag, axis=1) + y_off)
    y = jnp.concatenate(outs, axis=1) + dskip_ref[...] * xs

    zz = z_ref[0, rows, :].astype(F32)
    y = y * _silu(zz)
    normed = []
    for g in range(SSD_GROUPS):
        yg = y[:, g * gw:(g + 1) * gw]
        normed.append(yg * lax.rsqrt(jnp.mean(yg * yg, axis=-1, keepdims=True) + RMS_EPS))
    o_ref[0, rows, :] = (jnp.concatenate(normed, axis=1) * gn_ref[...]).astype(o_ref.dtype)


def _ssd(xbc, z, dtr, dt_tile, conv_w, conv_b, dt_bias, a_log, d_skip, ssd_norm):
    b, s, xdim = xbc.shape
    d_inner = z.shape[2]
    heads = a_log.shape[0]
    l = SSD_CHUNK
    pad = LANES - heads
    dtb = jnp.pad(dt_bias.astype(F32), (0, pad)).reshape(1, LANES)
    alog = jnp.pad(a_log.astype(F32), (0, pad)).reshape(1, LANES)
    expand = (jnp.arange(LANES)[:, None] == (jnp.arange(d_inner)[None, :] // SSD_HEAD_DIM)).astype(BF16)
    dskip = jnp.repeat(d_skip.astype(F32), SSD_HEAD_DIM).reshape(1, d_inner)
    step = jnp.arange(l)
    shift = jnp.concatenate([(jnp.arange(2 * l)[None, :] == (l + step - lag)[:, None]).astype(BF16)
                             for lag in range(1, CONV_WIDTH)], axis=0)
    kern = functools.partial(_ssd_kernel, d_inner=d_inner, heads=heads)
    rows = SSD_SUB * l
    return pl.pallas_call(
        kern,
        grid=(b, s // rows),
        in_specs=[pl.BlockSpec((1, rows, xdim), lambda i, c: (i, c, 0)),
                  pl.BlockSpec((1, rows, d_inner), lambda i, c: (i, c, 0)),
                  pl.BlockSpec((1, rows, LANES), lambda i, c: (i, c, dt_tile)),
                  _const_spec((CONV_WIDTH, xdim)),
                  _const_spec((1, xdim)),
                  _const_spec(((CONV_WIDTH - 1) * l, 2 * l)),
                  _const_spec((1, LANES)),
                  _const_spec((1, LANES)),
                  _const_spec((LANES, d_inner)),
                  _const_spec((1, d_inner)),
                  _const_spec((1, d_inner))],
        out_specs=pl.BlockSpec((1, rows, d_inner), lambda i, c: (i, c, 0)),
        out_shape=jax.ShapeDtypeStruct((b, s, d_inner), BF16),
        scratch_shapes=[pltpu.VMEM((l + rows, xdim), BF16),
                        pltpu.VMEM((SSD_GROUPS, SSD_STATE, d_inner // SSD_GROUPS), F32)],
        compiler_params=_params(2),
        name="ssd_scan",
    )(xbc, z, dtr, conv_w.astype(F32), conv_b.astype(F32).reshape(1, xdim), shift, dtb, alog, expand, dskip,
      ssd_norm.astype(F32).reshape(1, d_inner))


def _mla_prep_kernel(a_ref, pos_ref, invf_ref, qg_ref, kg_ref, wq1_ref, wq2_ref, wk_ref, wv_ref,
                     q_ref, k_ref, v_ref, *, q_rank, kv_rank, scale):
    a = a_ref[...]
    qa = a[:, :q_rank]
    ckv = a[:, q_rank:q_rank + kv_rank]
    rest = a[:, q_rank + kv_rank:]
    qn = (qa * lax.rsqrt(jnp.mean(qa * qa, axis=-1, keepdims=True) + RMS_EPS) * qg_ref[...]).astype(BF16)
    cn = (ckv * lax.rsqrt(jnp.mean(ckv * ckv, axis=-1, keepdims=True) + RMS_EPS) * kg_ref[...]).astype(BF16)

    ang = pos_ref[...].astype(F32) * invf_ref[...]
    cos = jnp.cos(ang)
    sin = jnp.sin(ang)
    q = jnp.dot(qn, wq1_ref[...], preferred_element_type=F32)
    q_rot = jnp.dot(qn, wq2_ref[...], preferred_element_type=F32)

    lane = lax.broadcasted_iota(jnp.int32, ang.shape, 1)
    first_half = lane < QK_NOPE + QK_ROPE // 2
    kpe = pltpu.roll(rest, QK_NOPE, 1)
    kpe_rot = jnp.where(first_half, -pltpu.roll(kpe, LANES - QK_ROPE // 2, 1), pltpu.roll(kpe, QK_ROPE // 2, 1))
    kpe = kpe * cos + kpe_rot * sin
    k_nope = jnp.dot(cn, wk_ref[...], preferred_element_type=F32)
    for h in range(MLA_HEADS):
        cols = slice(h * LANES, (h + 1) * LANES)
        q_ref[:, cols] = ((q[:, cols] * cos + q_rot[:, cols] * sin) * scale).astype(q_ref.dtype)
        k_ref[:, cols] = (k_nope[:, cols] + kpe).astype(k_ref.dtype)

    v = jnp.dot(cn, wv_ref[...], preferred_element_type=F32)
    lane = lax.broadcasted_iota(jnp.int32, v.shape, 1) % LANES
    v_ref[...] = jnp.where(lane == V_HEAD, 1.0, v).astype(v_ref.dtype)


def _mla_prep(qkv_a, width, positions, q_a_norm, w_q_b, kv_a_norm, w_kv_b, tm):
    t = qkv_a.shape[0]
    q_rank, kv_rank = w_q_b.shape[0], w_kv_b.shape[0]
    hd = MLA_HEADS * LANES
    qk_head = QK_NOPE + QK_ROPE
    half = QK_ROPE // 2
    tm = min(tm, t)

    w3 = w_q_b.astype(F32).reshape(q_rank, MLA_HEADS, qk_head)
    wq1 = jnp.pad(w3, ((0, 0), (0, 0), (0, LANES - qk_head))).reshape(q_rank, hd).astype(BF16)
    pe = w3[..., QK_NOPE:]
    rot = jnp.concatenate([-pe[..., half:], pe[..., :half]], axis=-1)
    wq2 = jnp.pad(rot, ((0, 0), (0, 0), (QK_NOPE, LANES - qk_head))).reshape(q_rank, hd).astype(BF16)
    kv3 = w_kv_b.astype(F32).reshape(kv_rank, MLA_HEADS, QK_NOPE + V_HEAD)
    wk = jnp.pad(kv3[..., :QK_NOPE], ((0, 0), (0, 0), (0, LANES - QK_NOPE))).reshape(kv_rank, hd).astype(BF16)
    wv = jnp.pad(kv3[..., QK_NOPE:], ((0, 0), (0, 0), (0, LANES - V_HEAD))).reshape(kv_rank, hd).astype(BF16)

    inv_freq = ROPE_THETA ** (-jnp.arange(half, dtype=F32) / half)
    invf = jnp.concatenate([jnp.zeros((QK_NOPE,), F32), inv_freq, inv_freq,
                            jnp.zeros((LANES - qk_head,), F32)]).reshape(1, LANES)

    kern = functools.partial(_mla_prep_kernel, q_rank=q_rank, kv_rank=kv_rank,
                             scale=qk_head ** -0.5 * LOG2_E)
    out = jax.ShapeDtypeStruct((t, hd), BF16)
    return pl.pallas_call(
        kern,
        grid=(t // tm,),
        in_specs=[pl.BlockSpec((tm, width), lambda i: (i, 0)),
                  pl.BlockSpec((tm, 1), lambda i: (i, 0)),
                  _const_spec((1, LANES)),
                  _const_spec((1, q_rank)),
                  _const_spec((1, kv_rank)),
                  _const_spec((q_rank, hd)),
                  _const_spec((q_rank, hd)),
                  _const_spec((kv_rank, hd)),
                  _const_spec((kv_rank, hd))],
        out_specs=[pl.BlockSpec((tm, hd), lambda i: (i, 0))] * 3,
        out_shape=[out, out, out],
        compiler_params=_params(1),
        name="mla_prep",
    )(qkv_a, positions.reshape(t, 1), invf, q_a_norm.astype(F32).reshape(1, q_rank),
      kv_a_norm.astype(F32).reshape(1, kv_rank), wq1, wq2, wk, wv)


def _flash_kernel(q_ref, k_ref, v_ref, o_ref, s_ref, m_ref, acc_ref, *, tq, tk, hp):
    i = pl.program_id(2)
    diag_blocks = tq // tk
    assert diag_blocks == 2
    n_tiles = tk // LANES
    m_ref[...] = jnp.full(m_ref.shape, NEG, F32)
    acc_ref[...] = jnp.zeros(acc_ref.shape, F32)

    def rows(j):
        return pl.ds(pl.multiple_of(j * tk, tk), tk)

    def head(h):
        return slice(h * LANES, (h + 1) * LANES)

    def scores(j, slot, row0=0):
        for h in range(hp):
            s_ref[2 * h + slot, row0:, :] = lax.dot_general(
                q_ref[0, row0:, head(h)], k_ref[0, rows(j), head(h)],
                (((1,), (1,)), ((), ())), preferred_element_type=F32)

    def update(j, slot, key_offset=None, row0=0):
        nr = tq - row0
        for h in range(hp):
            tiles = [s_ref[2 * h + slot, row0:, c * LANES:(c + 1) * LANES] for c in range(n_tiles)]
            if key_offset is not None:
                r = lax.broadcasted_iota(jnp.int32, (nr, LANES), 0) + row0
                c0 = lax.broadcasted_iota(jnp.int32, (nr, LANES), 1)
                tiles = [jnp.where(c0 + (key_offset + c * LANES) <= r, t, NEG) for c, t in enumerate(tiles)]
            m_old = m_ref[h, row0:, :]
            m_new = jnp.maximum(m_old, jnp.max(functools.reduce(jnp.maximum, tiles), axis=-1, keepdims=True))
            p = jnp.concatenate([jnp.exp2(t - m_new) for t in tiles], axis=1).astype(BF16)
            acc_ref[h, row0:, :] = (jnp.exp2(m_old - m_new) * acc_ref[h, row0:, :]
                                    + jnp.dot(p, v_ref[0, rows(j), head(h)], preferred_element_type=F32))
            m_ref[h, row0:, :] = m_new

    scores(0, 0)

    def two_blocks(j):
        scores(j + 1, 1)
        update(j, 0)
        scores(j + 2, 0)
        update(j + 1, 1)

    def body(jj, carry):
        two_blocks(4 * jj)
        two_blocks(4 * jj + 2)
        return carry

    lax.fori_loop(0, i // 2, body, 0)

    @pl.when(i % 2 == 1)
    def _():
        two_blocks(2 * i - 2)

    scores(2 * i + 1, 1, row0=tk)
    update(2 * i, 0, key_offset=0)
    update(2 * i + 1, 1, key_offset=tk, row0=tk)
    outs = []
    for h in range(hp):
        acc = acc_ref[h]
        outs.append(acc[:, :V_HEAD] / acc[:, V_HEAD:V_HEAD + 1])
    o_ref[0] = jnp.concatenate(outs, axis=1).astype(o_ref.dtype)


def _flash(q, k, v, tq, tk, hp):
    b, s, hd = q.shape
    w = hp * LANES
    return pl.pallas_call(
        functools.partial(_flash_kernel, tq=tq, tk=tk, hp=hp),
        grid=(b, hd // w, s // tq),
        in_specs=[pl.BlockSpec((1, tq, w), lambda bi, h, i: (bi, i, h)),
                  pl.BlockSpec((1, s, w), lambda bi, h, i: (bi, 0, h)),
                  pl.BlockSpec((1, s, w), lambda bi, h, i: (bi, 0, h))],
        out_specs=pl.BlockSpec((1, tq, hp * V_HEAD), lambda bi, h, i: (bi, i, h)),
        out_shape=jax.ShapeDtypeStruct((b, s, (hd // LANES) * V_HEAD), BF16),
        scratch_shapes=[pltpu.VMEM((2 * hp, tq, tk), F32),
                        pltpu.VMEM((hp, tq, LANES), F32), pltpu.VMEM((hp, tq, LANES), F32)],
        compiler_params=_params(3),
        name="mla_flash",
    )(q, k, v)


def _merge_kernel(yn_ref, o_ref, g_ref, x_ref, wso_ref, wmo_ref, wo_ref, h_ref, *, d_model):
    y_ssd = jnp.dot(yn_ref[...], wso_ref[...], preferred_element_type=F32)
    y_mla = jnp.dot(o_ref[...], wmo_ref[...], preferred_element_type=F32)
    g = g_ref[...].astype(F32)
    merged = _sigmoid(g[:, :d_model]) * y_ssd + _sigmoid(g[:, d_model:]) * y_mla
    h_ref[...] = x_ref[...] + jnp.dot(merged.astype(BF16), wo_ref[...], preferred_element_type=F32)


def _merge(yn, o, gates, x, w_ssd_out, w_mla_out, w_o, tm):
    t, d_model = x.shape
    tm = min(tm, t)
    row = lambda width: pl.BlockSpec((tm, width), lambda i: (i, 0))
    return pl.pallas_call(
        functools.partial(_merge_kernel, d_model=d_model),
        grid=(t // tm,),
        in_specs=[row(yn.shape[1]), row(o.shape[1]), row(gates.shape[1]), row(d_model),
                  _const_spec(w_ssd_out.shape), _const_spec(w_mla_out.shape), _const_spec(w_o.shape)],
        out_specs=row(d_model),
        out_shape=jax.ShapeDtypeStruct((t, d_model), F32),
        compiler_params=_params(1),
        name="merge",
    )(yn, o, gates, x, w_ssd_out, w_mla_out, w_o)


def _xattn_router_kernel(h_ref, kv_ref, gx_ref, gm_ref, wq_ref, wo_ref, wrh_ref, wrl_ref, br_ref,
                         h2_ref, hm_ref, comb_ref, *, d_model):
    h1 = h_ref[...]
    hn = (h1 * lax.rsqrt(jnp.mean(h1 * h1, axis=-1, keepdims=True) + RMS_EPS) * gx_ref[...]).astype(BF16)
    q = jnp.dot(hn, wq_ref[...], preferred_element_type=F32).astype(BF16)
    hd = d_model // XA_HEADS
    kv = kv_ref[0]
    outs = []
    for hh in range(XA_HEADS):
        qh = q[:, hh * hd:(hh + 1) * hd]
        kh = kv[:, hh * hd:(hh + 1) * hd]
        vh = kv[:, d_model + hh * hd:d_model + (hh + 1) * hd]
        s = lax.dot_general(qh, kh, (((1,), (1,)), ((), ())), preferred_element_type=F32) * (hd ** -0.5)
        p = jnp.exp(s - jnp.max(s, axis=-1, keepdims=True))
        p = p / jnp.sum(p, axis=-1, keepdims=True)
        outs.append(jnp.dot(p.astype(BF16), vh, preferred_element_type=F32))
    ox = jnp.concatenate(outs, axis=1).astype(BF16)
    h2 = h1 + jnp.dot(ox, wo_ref[...], preferred_element_type=F32)
    h2_ref[...] = h2

    hm = h2 * lax.rsqrt(jnp.mean(h2 * h2, axis=-1, keepdims=True) + RMS_EPS) * gm_ref[...]
    hm_ref[...] = hm.astype(hm_ref.dtype)

    hi, lo = _split_bf16(hm)
    logits = (jnp.dot(hi, wrh_ref[...], preferred_element_type=F32)
              + jnp.dot(lo, wrh_ref[...], preferred_element_type=F32)
              + jnp.dot(hi, wrl_ref[...], preferred_element_type=F32)) + br_ref[...]
    gl = logits[:, :LANES]
    el = logits[:, LANES:]
    lane = lax.broadcasted_iota(jnp.int32, gl.shape, 1)

    def first_argmax(v, vmax):
        return jnp.min(jnp.where(v == vmax, lane, LANES), axis=-1, keepdims=True)

    gmax = jnp.max(gl, axis=-1, keepdims=True)
    g_sel = first_argmax(gl, gmax)
    g_w = 1.0 / jnp.sum(jnp.exp(gl - gmax), axis=-1, keepdims=True)
    lo_lane = g_sel * EXPERTS_PER_GROUP
    in_group = (lane >= lo_lane) & (lane < lo_lane + EXPERTS_PER_GROUP)
    e1 = jnp.where(in_group, el, NEG)
    v1 = jnp.max(e1, axis=-1, keepdims=True)
    i1 = first_argmax(e1, v1)
    e2 = jnp.where(lane == i1, NEG, e1)
    v2 = jnp.max(e2, axis=-1, keepdims=True)
    i2 = first_argmax(e2, v2)
    r = jnp.exp(v2 - v1)
    w1 = g_w / (1.0 + r)
    w2 = g_w * r / (1.0 + r)
    comb_ref[...] = jnp.where(lane == i1, w1, 0.0) + jnp.where(lane == i2, w2, 0.0)


def _xattn_router(h1, kvx, tokens_per_batch, norm_xattn, norm_moe, w_xq, w_xo, wr_hi, wr_lo, b_r, tm):
    t, d_model = h1.shape
    tm = min(tm, tokens_per_batch)
    per_b = tokens_per_batch // tm
    n_mem = kvx.shape[1]
    row = lambda width: pl.BlockSpec((tm, width), lambda i: (i, 0))
    return pl.pallas_call(
        functools.partial(_xattn_router_kernel, d_model=d_model),
        grid=(t // tm,),
        in_specs=[row(d_model),
                  pl.BlockSpec((1, n_mem, 2 * d_model), lambda i: (i // per_b, 0, 0)),
                  _const_spec((1, d_model)), _const_spec((1, d_model)),
                  _const_spec(w_xq.shape), _const_spec(w_xo.shape),
                  _const_spec(wr_hi.shape), _const_spec(wr_lo.shape), _const_spec(b_r.shape)],
        out_specs=[row(d_model), row(d_model), row(LANES)],
        out_shape=[jax.ShapeDtypeStruct((t, d_model), F32),
                   jax.ShapeDtypeStruct((t, d_model), BF16),
                   jax.ShapeDtypeStruct((t, LANES), F32)],
        compiler_params=_params(1),
        name="xattn_router",
    )(h1, kvx, norm_xattn.astype(F32).reshape(1, d_model), norm_moe.astype(F32).reshape(1, d_model),
      w_xq, w_xo, wr_hi, wr_lo, b_r)


def _moe_kernel(hm_ref, comb_ref, h2_ref, upper_ref, gmap_ref, wg_ref, wu_ref, wd_ref, gf_ref, o_ref,
                acc_ref, mem_ref, rank_ref, mem_t_ref, rank_t_ref, *, final_norm, cap):
    g = pl.program_id(1)
    tm = hm_ref.shape[0]
    n_local = wg_ref.shape[0]
    comb = comb_ref[...]
    lane = lax.broadcasted_iota(jnp.int32, (tm, LANES), 1)

    @pl.when(g == 0)
    def _():
        acc_ref[...] = jnp.zeros(acc_ref.shape, F32)
        chosen = (comb > 0.0).astype(BF16)
        member = (jnp.dot(chosen, gmap_ref[...], preferred_element_type=F32) > 0.0).astype(F32)
        member_t = member.T
        rank_t = jnp.dot(member_t.astype(BF16), upper_ref[...], preferred_element_type=F32)
        mem_ref[...] = member
        mem_t_ref[...] = member_t
        rank_t_ref[...] = rank_t
        rank_ref[...] = rank_t.T

    mem_row = mem_t_ref[pl.ds(g, 1), :]
    rank_row = rank_t_ref[pl.ds(g, 1), :]
    pick = lane == g
    mem_col = jnp.sum(jnp.where(pick, mem_ref[...], 0.0), axis=-1, keepdims=True)
    rank_col = jnp.sum(jnp.where(pick, rank_ref[...], 0.0), axis=-1, keepdims=True)
    count = jnp.sum(mem_row).astype(jnp.int32)
    comb_hi, comb_lo = _split_bf16(comb)
    hm = hm_ref[...]

    def chunk(c, carry):
        base = (c * cap).astype(F32)
        slot_r = lax.broadcasted_iota(jnp.int32, (cap, tm), 0).astype(F32) + base
        sel = ((rank_row == slot_r) & (mem_row > 0.0)).astype(BF16)
        xc = jnp.dot(sel, hm, preferred_element_type=F32).astype(BF16)
        wc = (jnp.dot(sel, comb_hi, preferred_element_type=F32)
              + jnp.dot(sel, comb_lo, preferred_element_type=F32))
        lane_c = lax.broadcasted_iota(jnp.int32, (cap, LANES), 1)
        yc = jnp.zeros((cap, hm.shape[1]), F32)
        for e in range(n_local):
            w = jnp.sum(jnp.where(lane_c == g * n_local + e, wc, 0.0), axis=-1, keepdims=True)
            hg = jnp.dot(xc, wg_ref[e], preferred_element_type=F32)
            hu = jnp.dot(xc, wu_ref[e], preferred_element_type=F32)
            act = (_silu(hg) * hu * w).astype(BF16)
            yc = yc + jnp.dot(act, wd_ref[e], preferred_element_type=F32)
        slot_c = lax.broadcasted_iota(jnp.int32, (tm, cap), 1).astype(F32) + base
        back = ((rank_col == slot_c) & (mem_col > 0.0)).astype(BF16)
        acc_ref[...] += jnp.dot(back, yc.astype(BF16), preferred_element_type=F32)
        return carry

    lax.fori_loop(0, (count + cap - 1) // cap, chunk, 0)

    @pl.when(g == pl.num_programs(1) - 1)
    def _():
        h3 = h2_ref[...] + acc_ref[...]
        if final_norm:
            h3 = h3 * lax.rsqrt(jnp.mean(h3 * h3, axis=-1, keepdims=True) + RMS_EPS) * gf_ref[...]
        o_ref[...] = h3


def _moe(hm, comb, h2, w_gate, w_up, w_down, norm_final, final_norm, tm, cap):
    t, d_model = hm.shape
    n_exp, _, d_exp = w_gate.shape
    n_local = EXPERTS_PER_GROUP
    n_groups = n_exp // n_local
    tm = min(tm, t)
    cap = min(cap, tm)
    idx = jnp.arange(tm)
    upper = (idx[:, None] < idx[None, :]).astype(BF16)
    lanes = jnp.arange(LANES)
    gmap = ((lanes[:, None] // n_local == lanes[None, :]) & (lanes[:, None] < n_exp)).astype(BF16)
    row = lambda width, **kw: pl.BlockSpec((tm, width), lambda i, g: (i, 0), **kw)
    once = dict(pipeline_mode=pl.Buffered(1))
    return pl.pallas_call(
        functools.partial(_moe_kernel, final_norm=final_norm, cap=cap),
        grid=(t // tm, n_groups),
        in_specs=[row(d_model), row(LANES), row(d_model, **once),
                  pl.BlockSpec((tm, tm), lambda i, g: (0, 0), **once),
                  pl.BlockSpec((LANES, LANES), lambda i, g: (0, 0), **once),
                  pl.BlockSpec((n_local, d_model, d_exp), lambda i, g: (g, 0, 0)),
                  pl.BlockSpec((n_local, d_model, d_exp), lambda i, g: (g, 0, 0)),
                  pl.BlockSpec((n_local, d_exp, d_model), lambda i, g: (g, 0, 0)),
                  pl.BlockSpec((1, d_model), lambda i, g: (0, 0))],
        out_specs=row(d_model, **once),
        out_shape=jax.ShapeDtypeStruct((t, d_model), F32),
        scratch_shapes=[pltpu.VMEM((tm, d_model), F32),
                        pltpu.VMEM((tm, LANES), F32), pltpu.VMEM((tm, LANES), F32),
                        pltpu.VMEM((LANES, tm), F32), pltpu.VMEM((LANES, tm), F32)],
        compiler_params=_params(2),
        name="moe_experts",
    )(hm, comb, h2, upper, gmap, w_gate, w_up, w_down, norm_final.astype(F32).reshape(1, d_model))


def kernel(x, mem, positions, norm_mix, w_in, conv_w, conv_b, dt_bias, a_log, d_skip, ssd_norm, w_ssd_out, q_a_norm, w_q_b, kv_a_norm, w_kv_b, w_mla_out, w_o, norm_xattn, norm_mem, w_xq, w_xkv, w_xo, norm_moe, w_router_group, b_router_group, w_router_expert, b_router_expert, w_exp_gate, w_exp_up, w_exp_down, norm_final):
    b, s, d_model = x.shape
    t = b * s
    n_mem = mem.shape[1]
    depth = norm_mix.shape[0]
    d_inner = ssd_norm.shape[1]
    xbc_dim = conv_w.shape[2]
    heads = a_log.shape[1]
    q_rank, kv_rank = w_q_b.shape[1], w_kv_b.shape[1]
    n_groups = w_router_group.shape[2]
    n_exp = w_router_expert.shape[2]

    o_z, o_xbc = 0, d_inner
    o_dt = o_xbc + xbc_dim
    o_qa = o_dt + heads
    o_gs = o_qa + q_rank + kv_rank + QK_ROPE
    o_end = o_gs + 2 * d_model

    h = x.reshape(t, d_model).astype(F32)
    for i in range(depth):
        wi = w_in[i]
        w_z = wi[:, o_z:o_xbc].astype(BF16)
        w_xbc = wi[:, o_xbc:o_dt].astype(BF16)
        qkv_width = q_rank + kv_rank + LANES
        w_small = jnp.concatenate([jnp.pad(wi[:, o_qa:o_gs], ((0, 0), (0, LANES - QK_ROPE))),
                                   jnp.pad(wi[:, o_dt:o_qa], ((0, 0), (0, LANES - heads)))], axis=1).astype(BF16)
        w_gates = wi[:, o_gs:o_end].astype(BF16)

        zb = _norm_matmul(h, norm_mix[i], w_z, BF16, 1024, d_inner).reshape(b, s, d_inner)
        xbc = _norm_matmul(h, norm_mix[i], w_xbc, BF16, 1024, xbc_dim).reshape(b, s, xbc_dim)
        gates = _norm_matmul(h, norm_mix[i], w_gates, BF16, 1024, 2 * d_model)
        small = _norm_matmul(h, norm_mix[i], w_small, F32, 1024, w_small.shape[1])

        yn = _ssd(xbc, zb, small.reshape(b, s, w_small.shape[1]), qkv_width // LANES, conv_w[i], conv_b[i],
                  dt_bias[i], a_log[i], d_skip[i], ssd_norm[i])

        hd = MLA_HEADS * LANES
        q, k, v = _mla_prep(small, qkv_width, positions, q_a_norm[i], w_q_b[i], kv_a_norm[i], w_kv_b[i], 512)
        o = _flash(q.reshape(b, s, hd), k.reshape(b, s, hd), v.reshape(b, s, hd), FLASH_TQ, FLASH_TK, FLASH_HP)

        h1 = _merge(yn.reshape(t, d_inner), o.reshape(t, MLA_HEADS * V_HEAD), gates, h, w_ssd_out[i].astype(BF16),
                    w_mla_out[i].astype(BF16), w_o[i].astype(BF16), 512)

        kvx = _norm_matmul(mem.reshape(b * n_mem, d_model).astype(F32), norm_mem[i], w_xkv[i].astype(BF16), BF16,
                           b * n_mem, 1024).reshape(b, n_mem, 2 * d_model)
        w_r = jnp.concatenate([jnp.pad(w_router_group[i].astype(F32), ((0, 0), (0, LANES - n_groups))),
                               jnp.pad(w_router_expert[i].astype(F32), ((0, 0), (0, LANES - n_exp)))], axis=1)
        wr_hi = w_r.astype(BF16)
        wr_lo = (w_r - wr_hi.astype(F32)).astype(BF16)
        b_r = jnp.concatenate([jnp.pad(b_router_group[i].astype(F32), (0, LANES - n_groups), constant_values=NEG),
                               jnp.pad(b_router_expert[i].astype(F32), (0, LANES - n_exp), constant_values=NEG)]
                              ).reshape(1, 2 * LANES)
        h2, hm, comb = _xattn_router(h1, kvx, s, norm_xattn[i], norm_moe[i], w_xq[i].astype(BF16),
                                     w_xo[i].astype(BF16), wr_hi, wr_lo, b_r, 512)

        h = _moe(hm, comb, h2, w_exp_gate[i].astype(BF16), w_exp_up[i].astype(BF16), w_exp_down[i].astype(BF16),
                 norm_final, i == depth - 1, MOE_TM, MOE_CAP)
    return h.reshape(b, s, d_model)
```

```python
import functools

import jax
import jax.numpy as jnp
from jax import lax
from jax.experimental import pallas as pl
from jax.experimental.pallas import tpu as pltpu

F32 = jnp.float32
BF16 = jnp.bfloat16
RMS_EPS = 1e-6
NEG = -1e30
ROPE_THETA = 10000.0
LOG2_E = 1.4426950408889634

LANES = 128
SSD_CHUNK = 128
SSD_SUB = 2
SSD_HEAD_DIM = 64
SSD_GROUPS = 4
SSD_STATE = 128
CONV_WIDTH = 4
MLA_HEADS = 16
QK_NOPE = 64
QK_ROPE = 32
V_HEAD = 64
XA_HEADS = 4
EXPERTS_PER_GROUP = 8
VMEM_LIMIT = 56 * 1024 * 1024
FLASH_TQ = 1024
FLASH_TK = 512
FLASH_HP = 2
XATTN_TM = 1024
XATTN_SUB_ROWS = 512
MOE_TM = 1024
MOE_CAP = 288


def _params(n_axes):
    return pltpu.CompilerParams(dimension_semantics=("arbitrary",) * n_axes,
                                vmem_limit_bytes=VMEM_LIMIT)


def _sigmoid(v):
    return 1.0 / (1.0 + jnp.exp(-v))


def _silu(v):
    return v * _sigmoid(v)


def _split_bf16(v):
    hi = v.astype(BF16)
    lo = (v - hi.astype(F32)).astype(BF16)
    return hi, lo


def _const_spec(shape):
    return pl.BlockSpec(shape, lambda *_: (0,) * len(shape))


def _norm_matmul_kernel(x_ref, g_ref, w_ref, o_ref, un_ref):
    @pl.when(pl.program_id(1) == 0)
    def _():
        x = x_ref[...]
        r = lax.rsqrt(jnp.mean(x * x, axis=-1, keepdims=True) + RMS_EPS)
        un_ref[...] = (x * r * g_ref[...]).astype(BF16)

    o_ref[...] = jnp.dot(un_ref[...], w_ref[...], preferred_element_type=F32).astype(o_ref.dtype)


def _norm_matmul(x, gain, w, out_dtype, tm, tn):
    m, k = x.shape
    n = w.shape[1]
    tm, tn = min(tm, m), min(tn, n)
    return pl.pallas_call(
        _norm_matmul_kernel,
        grid=(m // tm, n // tn),
        in_specs=[pl.BlockSpec((tm, k), lambda i, j: (i, 0)),
                  pl.BlockSpec((1, k), lambda i, j: (0, 0)),
                  pl.BlockSpec((k, tn), lambda i, j: (0, j))],
        out_specs=pl.BlockSpec((tm, tn), lambda i, j: (i, j)),
        out_shape=jax.ShapeDtypeStruct((m, n), out_dtype),
        scratch_shapes=[pltpu.VMEM((tm, k), BF16)],
        compiler_params=_params(2),
        name="norm_matmul",
    )(x, gain.reshape(1, k), w)


def _ssd_kernel(xbc_ref, z_ref, dtr_ref, cw_ref, cb_ref, shift_ref, dtb_ref, alog_ref, expand_ref, dskip_ref,
                gn_ref, o_ref, cbuf_ref, state_ref, *, d_inner, heads):
    l = SSD_CHUNK
    n_sub = xbc_ref.shape[1] // l

    @pl.when(pl.program_id(1) == 0)
    def _():
        cbuf_ref[0:l, :] = jnp.zeros((l, cbuf_ref.shape[1]), cbuf_ref.dtype)
        state_ref[...] = jnp.zeros(state_ref.shape, F32)

    cbuf_ref[l:, :] = xbc_ref[0]
    for sub in range(n_sub):
        _ssd_chunk(sub, xbc_ref, z_ref, dtr_ref, cw_ref, cb_ref, shift_ref, dtb_ref, alog_ref, expand_ref,
                   dskip_ref, gn_ref, o_ref, cbuf_ref, state_ref, d_inner=d_inner, heads=heads)
    cbuf_ref[0:l, :] = xbc_ref[0, (n_sub - 1) * l:, :]


def _ssd_chunk(sub, xbc_ref, z_ref, dtr_ref, cw_ref, cb_ref, shift_ref, dtb_ref, alog_ref, expand_ref, dskip_ref,
               gn_ref, o_ref, cbuf_ref, state_ref, *, d_inner, heads):
    l = SSD_CHUNK
    gn = SSD_GROUPS * SSD_STATE
    gw = d_inner // SSD_GROUPS
    heads_per_group = heads // SSD_GROUPS
    rows = slice(sub * l, (sub + 1) * l)

    cur = xbc_ref[0, rows, :]
    shifted = jnp.dot(shift_ref[...], cbuf_ref[sub * l:(sub + 2) * l, :],
                      preferred_element_type=F32)
    conv = cb_ref[...] + cw_ref[CONV_WIDTH - 1:CONV_WIDTH, :] * cur.astype(F32)
    for lag in range(1, CONV_WIDTH):
        k = CONV_WIDTH - 1 - lag
        conv = conv + cw_ref[k:k + 1, :] * shifted[(lag - 1) * l:lag * l, :]
    conv = _silu(conv)
    xs = conv[:, :d_inner]
    bm = conv[:, d_inner:d_inner + gn]
    cm = conv[:, d_inner + gn:]

    dtv = dtr_ref[0, rows, :] + dtb_ref[...]
    dt = jnp.maximum(dtv, 0.0) + jnp.log1p(jnp.exp(-jnp.abs(dtv)))
    a_dt = dt * (-jnp.exp(alog_ref[...]))
    row_i = lax.broadcasted_iota(jnp.int32, (l, l), 0)
    col_i = lax.broadcasted_iota(jnp.int32, (l, l), 1)
    causal = col_i <= row_i
    tri = causal.astype(BF16)
    hi, lo = _split_bf16(a_dt)
    a_cum = jnp.dot(tri, hi, preferred_element_type=F32) + jnp.dot(tri, lo, preferred_element_type=F32)
    a_cum_t = a_cum.T

    ex = expand_ref[...]

    def expand(v):
        vh, vl = _split_bf16(v)
        return jnp.dot(vh, ex, preferred_element_type=F32) + jnp.dot(vl, ex, preferred_element_type=F32)

    dt_x = expand(dt)
    a_cum_x = expand(a_cum)
    last = a_cum_x[l - 1:l, :]
    grow = jnp.exp(a_cum_x)
    to_end = jnp.exp(last - a_cum_x)
    chunk_decay = jnp.exp(last)

    x_dt = xs * dt_x
    x_dt_b = x_dt.astype(BF16)
    x_end_b = (x_dt * to_end).astype(BF16)

    lane = lax.broadcasted_iota(jnp.int32, (l, LANES), 1)
    first_head = lane < SSD_HEAD_DIM
    outs = []
    for g in range(SSD_GROUPS):
        bg = bm[:, g * SSD_STATE:(g + 1) * SSD_STATE]
        cg = cm[:, g * SSD_STATE:(g + 1) * SSD_STATE].astype(BF16)
        cbm = lax.dot_general(cg, bg.astype(BF16), (((1,), (1,)), ((), ())), preferred_element_type=F32)
        cols = slice(g * gw, (g + 1) * gw)
        diag = []
        for pair in range(heads_per_group // 2):
            h0 = g * heads_per_group + 2 * pair
            xp = x_dt_b[:, h0 * SSD_HEAD_DIM:(h0 + 2) * SSD_HEAD_DIM]
            ys = []
            for h in (h0, h0 + 1):
                seg = a_cum[:, h:h + 1] - a_cum_t[h:h + 1, :]
                mh = (cbm * jnp.exp(jnp.where(causal, seg, NEG))).astype(BF16)
                ys.append(jnp.dot(mh, xp, preferred_element_type=F32))
            diag.append(jnp.where(first_head, ys[0], ys[1]))
        st = state_ref[g]
        y_off = jnp.dot(cg, st.astype(BF16), preferred_element_type=F32) * grow[:, cols]
        state_ref[g] = st * chunk_decay[:, cols] + jnp.dot(bg.T.astype(BF16), x_end_b[:, cols],
                                                          preferred_element_type=F32)
        outs.append(jnp.concatenate(diag, axis=1) + y_off)
    y = jnp.concatenate(outs, axis=1) + dskip_ref[...] * xs

    zz = z_ref[0, rows, :].astype(F32)
    y = y * _silu(zz)
    normed = []
    for g in range(SSD_GROUPS):
        yg = y[:, g * gw:(g + 1) * gw]
        normed.append(yg * lax.rsqrt(jnp.mean(yg * yg, axis=-1, keepdims=True) + RMS_EPS))
    o_ref[0, rows, :] = (jnp.concatenate(normed, axis=1) * gn_ref[...]).astype(o_ref.dtype)


def _ssd(xbc, z, dtr, dt_tile, conv_w, conv_b, dt_bias, a_log, d_skip, ssd_norm):
    b, s, xdim = xbc.shape
    d_inner = z.shape[2]
    heads = a_log.shape[0]
    l = SSD_CHUNK
    pad = LANES - heads
    dtb = jnp.pad(dt_bias.astype(F32), (0, pad)).reshape(1, LANES)
    alog = jnp.pad(a_log.astype(F32), (0, pad)).reshape(1, LANES)
    expand = (jnp.arange(LANES)[:, None] == (jnp.arange(d_inner)[None, :] // SSD_HEAD_DIM)).astype(BF16)
    dskip = jnp.repeat(d_skip.astype(F32), SSD_HEAD_DIM).reshape(1, d_inner)
    step = jnp.arange(l)
    shift = jnp.concatenate([(jnp.arange(2 * l)[None, :] == (l + step - lag)[:, None]).astype(BF16)
                             for lag in range(1, CONV_WIDTH)], axis=0)
    kern = functools.partial(_ssd_kernel, d_inner=d_inner, heads=heads)
    rows = SSD_SUB * l
    return pl.pallas_call(
        kern,
        grid=(b, s // rows),
        in_specs=[pl.BlockSpec((1, rows, xdim), lambda i, c: (i, c, 0)),
                  pl.BlockSpec((1, rows, d_inner), lambda i, c: (i, c, 0)),
                  pl.BlockSpec((1, rows, LANES), lambda i, c: (i, c, dt_tile)),
                  _const_spec((CONV_WIDTH, xdim)),
                  _const_spec((1, xdim)),
                  _const_spec(((CONV_WIDTH - 1) * l, 2 * l)),
                  _const_spec((1, LANES)),
                  _const_spec((1, LANES)),
                  _const_spec((LANES, d_inner)),
                  _const_spec((1, d_inner)),
                  _const_spec((1, d_inner))],
        out_specs=pl.BlockSpec((1, rows, d_inner), lambda i, c: (i, c, 0)),
        out_shape=jax.ShapeDtypeStruct((b, s, d_inner), BF16),
        scratch_shapes=[pltpu.VMEM((l + rows, xdim), BF16),
                        pltpu.VMEM((SSD_GROUPS, SSD_STATE, d_inner // SSD_GROUPS), F32)],
        compiler_params=_params(2),
        name="ssd_scan",
    )(xbc, z, dtr, conv_w.astype(F32), conv_b.astype(F32).reshape(1, xdim), shift, dtb, alog, expand, dskip,
      ssd_norm.astype(F32).reshape(1, d_inner))


def _mla_prep_kernel(a_ref, pos_ref, invf_ref, qg_ref, kg_ref, wq1_ref, wq2_ref, wk_ref, wv_ref,
                     q_ref, k_ref, v_ref, *, q_rank, kv_rank, scale):
    a = a_ref[...]
    qa = a[:, :q_rank]
    ckv = a[:, q_rank:q_rank + kv_rank]
    rest = a[:, q_rank + kv_rank:]
    qn = (qa * lax.rsqrt(jnp.mean(qa * qa, axis=-1, keepdims=True) + RMS_EPS) * qg_ref[...]).astype(BF16)
    cn = (ckv * lax.rsqrt(jnp.mean(ckv * ckv, axis=-1, keepdims=True) + RMS_EPS) * kg_ref[...]).astype(BF16)

    ang = pos_ref[...].astype(F32) * invf_ref[...]
    cos = jnp.cos(ang)
    sin = jnp.sin(ang)
    q = jnp.dot(qn, wq1_ref[...], preferred_element_type=F32)
    q_rot = jnp.dot(qn, wq2_ref[...], preferred_element_type=F32)

    lane = lax.broadcasted_iota(jnp.int32, ang.shape, 1)
    first_half = lane < QK_NOPE + QK_ROPE // 2
    kpe = pltpu.roll(rest, QK_NOPE, 1)
    kpe_rot = jnp.where(first_half, -pltpu.roll(kpe, LANES - QK_ROPE // 2, 1), pltpu.roll(kpe, QK_ROPE // 2, 1))
    kpe = kpe * cos + kpe_rot * sin
    k_nope = jnp.dot(cn, wk_ref[...], preferred_element_type=F32)
    for h in range(MLA_HEADS):
        cols = slice(h * LANES, (h + 1) * LANES)
        q_ref[:, cols] = ((q[:, cols] * cos + q_rot[:, cols] * sin) * scale).astype(q_ref.dtype)
        k_ref[:, cols] = (k_nope[:, cols] + kpe).astype(k_ref.dtype)

    v = jnp.dot(cn, wv_ref[...], preferred_element_type=F32)
    lane = lax.broadcasted_iota(jnp.int32, v.shape, 1) % LANES
    v_ref[...] = jnp.where(lane == V_HEAD, 1.0, v).astype(v_ref.dtype)


def _mla_prep(qkv_a, width, positions, q_a_norm, w_q_b, kv_a_norm, w_kv_b, tm):
    t = qkv_a.shape[0]
    q_rank, kv_rank = w_q_b.shape[0], w_kv_b.shape[0]
    hd = MLA_HEADS * LANES
    qk_head = QK_NOPE + QK_ROPE
    half = QK_ROPE // 2
    tm = min(tm, t)

    w3 = w_q_b.astype(F32).reshape(q_rank, MLA_HEADS, qk_head)
    wq1 = jnp.pad(w3, ((0, 0), (0, 0), (0, LANES - qk_head))).reshape(q_rank, hd).astype(BF16)
    pe = w3[..., QK_NOPE:]
    rot = jnp.concatenate([-pe[..., half:], pe[..., :half]], axis=-1)
    wq2 = jnp.pad(rot, ((0, 0), (0, 0), (QK_NOPE, LANES - qk_head))).reshape(q_rank, hd).astype(BF16)
    kv3 = w_kv_b.astype(F32).reshape(kv_rank, MLA_HEADS, QK_NOPE + V_HEAD)
    wk = jnp.pad(kv3[..., :QK_NOPE], ((0, 0), (0, 0), (0, LANES - QK_NOPE))).reshape(kv_rank, hd).astype(BF16)
    wv = jnp.pad(kv3[..., QK_NOPE:], ((0, 0), (0, 0), (0, LANES - V_HEAD))).reshape(kv_rank, hd).astype(BF16)

    inv_freq = ROPE_THETA ** (-jnp.arange(half, dtype=F32) / half)
    invf = jnp.concatenate([jnp.zeros((QK_NOPE,), F32), inv_freq, inv_freq,
                            jnp.zeros((LANES - qk_head,), F32)]).reshape(1, LANES)

    kern = functools.partial(_mla_prep_kernel, q_rank=q_rank, kv_rank=kv_rank,
                             scale=qk_head ** -0.5 * LOG2_E)
    out = jax.ShapeDtypeStruct((t, hd), BF16)
    return pl.pallas_call(
        kern,
        grid=(t // tm,),
        in_specs=[pl.BlockSpec((tm, width), lambda i: (i, 0)),
                  pl.BlockSpec((tm, 1), lambda i: (i, 0)),
                  _const_spec((1, LANES)),
                  _const_spec((1, q_rank)),
                  _const_spec((1, kv_rank)),
                  _const_spec((q_rank, hd)),
                  _const_spec((q_rank, hd)),
                  _const_spec((kv_rank, hd)),
                  _const_spec((kv_rank, hd))],
        out_specs=[pl.BlockSpec((tm, hd), lambda i: (i, 0))] * 3,
        out_shape=[out, out, out],
        compiler_params=_params(1),
        name="mla_prep",
    )(qkv_a, positions.reshape(t, 1), invf, q_a_norm.astype(F32).reshape(1, q_rank),
      kv_a_norm.astype(F32).reshape(1, kv_rank), wq1, wq2, wk, wv)


def _flash_kernel(q_ref, k_ref, v_ref, o_ref, s_ref, m_ref, acc_ref, *, tq, tk, hp):
    i = pl.program_id(2)
    diag_blocks = tq // tk
    assert diag_blocks == 2
    n_tiles = tk // LANES
    m_ref[...] = jnp.full(m_ref.shape, NEG, F32)
    acc_ref[...] = jnp.zeros(acc_ref.shape, F32)

    def rows(j):
        return pl.ds(pl.multiple_of(j * tk, tk), tk)

    def head(h):
        return slice(h * LANES, (h + 1) * LANES)

    def scores(j, slot, row0=0):
        for h in range(hp):
            s_ref[2 * h + slot, row0:, :] = lax.dot_general(
                q_ref[0, row0:, head(h)], k_ref[0, rows(j), head(h)],
                (((1,), (1,)), ((), ())), preferred_element_type=F32)

    def update(j, slot, key_offset=None, row0=0):
        nr = tq - row0
        for h in range(hp):
            tiles = [s_ref[2 * h + slot, row0:, c * LANES:(c + 1) * LANES] for c in range(n_tiles)]
            if key_offset is not None:
                r = lax.broadcasted_iota(jnp.int32, (nr, LANES), 0) + row0
                c0 = lax.broadcasted_iota(jnp.int32, (nr, LANES), 1)
                tiles = [jnp.where(c0 + (key_offset + c * LANES) <= r, t, NEG) for c, t in enumerate(tiles)]
            m_old = m_ref[h, row0:, :]
            m_new = jnp.maximum(m_old, jnp.max(functools.reduce(jnp.maximum, tiles), axis=-1, keepdims=True))
            p = jnp.concatenate([jnp.exp2(t - m_new) for t in tiles], axis=1).astype(BF16)
            acc_ref[h, row0:, :] = (jnp.exp2(m_old - m_new) * acc_ref[h, row0:, :]
                                    + jnp.dot(p, v_ref[0, rows(j), head(h)], preferred_element_type=F32))
            m_ref[h, row0:, :] = m_new

    scores(0, 0)

    def two_blocks(j):
        scores(j + 1, 1)
        update(j, 0)
        scores(j + 2, 0)
        update(j + 1, 1)

    def body(jj, carry):
        two_blocks(4 * jj)
        two_blocks(4 * jj + 2)
        return carry

    lax.fori_loop(0, i // 2, body, 0)

    @pl.when(i % 2 == 1)
    def _():
        two_blocks(2 * i - 2)

    scores(2 * i + 1, 1, row0=tk)
    update(2 * i, 0, key_offset=0)
    update(2 * i + 1, 1, key_offset=tk, row0=tk)
    outs = []
    for h in range(hp):
        acc = acc_ref[h]
        outs.append(acc[:, :V_HEAD] / acc[:, V_HEAD:V_HEAD + 1])
    o_ref[0] = jnp.concatenate(outs, axis=1).astype(o_ref.dtype)


def _flash(q, k, v, tq, tk, hp):
    b, s, hd = q.shape
    w = hp * LANES
    return pl.pallas_call(
        functools.partial(_flash_kernel, tq=tq, tk=tk, hp=hp),
        grid=(b, hd // w, s // tq),
        in_specs=[pl.BlockSpec((1, tq, w), lambda bi, h, i: (bi, i, h)),
                  pl.BlockSpec((1, s, w), lambda bi, h, i: (bi, 0, h)),
                  pl.BlockSpec((1, s, w), lambda bi, h, i: (bi, 0, h))],
        out_specs=pl.BlockSpec((1, tq, hp * V_HEAD), lambda bi, h, i: (bi, i, h)),
        out_shape=jax.ShapeDtypeStruct((b, s, (hd // LANES) * V_HEAD), BF16),
        scratch_shapes=[pltpu.VMEM((2 * hp, tq, tk), F32),
                        pltpu.VMEM((hp, tq, LANES), F32), pltpu.VMEM((hp, tq, LANES), F32)],
        compiler_params=_params(3),
        name="mla_flash",
    )(q, k, v)


def _merge_kernel(yn_ref, o_ref, g_ref, x_ref, wso_ref, wmo_ref, wo_ref, h_ref, *, d_model):
    y_ssd = jnp.dot(yn_ref[...], wso_ref[...], preferred_element_type=F32)
    y_mla = jnp.dot(o_ref[...], wmo_ref[...], preferred_element_type=F32)
    g = g_ref[...].astype(F32)
    merged = _sigmoid(g[:, :d_model]) * y_ssd + _sigmoid(g[:, d_model:]) * y_mla
    h_ref[...] = x_ref[...] + jnp.dot(merged.astype(BF16), wo_ref[...], preferred_element_type=F32)


def _merge(yn, o, gates, x, w_ssd_out, w_mla_out, w_o, tm):
    t, d_model = x.shape
    tm = min(tm, t)
    row = lambda width: pl.BlockSpec((tm, width), lambda i: (i, 0))
    return pl.pallas_call(
        functools.partial(_merge_kernel, d_model=d_model),
        grid=(t // tm,),
        in_specs=[row(yn.shape[1]), row(o.shape[1]), row(gates.shape[1]), row(d_model),
                  _const_spec(w_ssd_out.shape), _const_spec(w_mla_out.shape), _const_spec(w_o.shape)],
        out_specs=row(d_model),
        out_shape=jax.ShapeDtypeStruct((t, d_model), F32),
        compiler_params=_params(1),
        name="merge",
    )(yn, o, gates, x, w_ssd_out, w_mla_out, w_o)


def _xattn_router_kernel(h_ref, kv_ref, gx_ref, gm_ref, wq_ref, wo_ref, wrh_ref, wrl_ref, br_ref,
                         h2_ref, hm_ref, comb_ref, *, d_model, sub_rows):
    for start in range(0, h_ref.shape[0], sub_rows):
        rows = slice(start, start + sub_rows)
        _xattn_router_rows(rows, h_ref, kv_ref, gx_ref, gm_ref, wq_ref, wo_ref, wrh_ref, wrl_ref, br_ref,
                           h2_ref, hm_ref, comb_ref, d_model=d_model)


def _xattn_router_rows(rows, h_ref, kv_ref, gx_ref, gm_ref, wq_ref, wo_ref, wrh_ref, wrl_ref, br_ref,
                       h2_ref, hm_ref, comb_ref, *, d_model):
    h1 = h_ref[rows, :]
    hn = (h1 * lax.rsqrt(jnp.mean(h1 * h1, axis=-1, keepdims=True) + RMS_EPS) * gx_ref[...]).astype(BF16)
    q = jnp.dot(hn, wq_ref[...], preferred_element_type=F32).astype(BF16)
    hd = d_model // XA_HEADS
    kv = kv_ref[0]
    outs = []
    for hh in range(XA_HEADS):
        qh = q[:, hh * hd:(hh + 1) * hd]
        kh = kv[:, hh * hd:(hh + 1) * hd]
        vh = kv[:, d_model + hh * hd:d_model + (hh + 1) * hd]
        s = lax.dot_general(qh, kh, (((1,), (1,)), ((), ())), preferred_element_type=F32) * (hd ** -0.5)
        p = jnp.exp(s - jnp.max(s, axis=-1, keepdims=True))
        p = p / jnp.sum(p, axis=-1, keepdims=True)
        outs.append(jnp.dot(p.astype(BF16), vh, preferred_element_type=F32))
    ox = jnp.concatenate(outs, axis=1).astype(BF16)
    h2 = h1 + jnp.dot(ox, wo_ref[...], preferred_element_type=F32)
    h2_ref[rows, :] = h2

    hm = h2 * lax.rsqrt(jnp.mean(h2 * h2, axis=-1, keepdims=True) + RMS_EPS) * gm_ref[...]
    hm_ref[rows, :] = hm.astype(hm_ref.dtype)

    hi, lo = _split_bf16(hm)
    logits = (jnp.dot(hi, wrh_ref[...], preferred_element_type=F32)
              + jnp.dot(lo, wrh_ref[...], preferred_element_type=F32)
              + jnp.dot(hi, wrl_ref[...], preferred_element_type=F32)) + br_ref[...]
    gl = logits[:, :LANES]
    el = logits[:, LANES:]
    lane = lax.broadcasted_iota(jnp.int32, gl.shape, 1)

    def first_argmax(v, vmax):
        return jnp.min(jnp.where(v == vmax, lane, LANES), axis=-1, keepdims=True)

    gmax = jnp.max(gl, axis=-1, keepdims=True)
    g_sel = first_argmax(gl, gmax)
    g_w = 1.0 / jnp.sum(jnp.exp(gl - gmax), axis=-1, keepdims=True)
    lo_lane = g_sel * EXPERTS_PER_GROUP
    in_group = (lane >= lo_lane) & (lane < lo_lane + EXPERTS_PER_GROUP)
    e1 = jnp.where(in_group, el, NEG)
    v1 = jnp.max(e1, axis=-1, keepdims=True)
    i1 = first_argmax(e1, v1)
    e2 = jnp.where(lane == i1, NEG, e1)
    v2 = jnp.max(e2, axis=-1, keepdims=True)
    i2 = first_argmax(e2, v2)
    r = jnp.exp(v2 - v1)
    w1 = g_w / (1.0 + r)
    w2 = g_w * r / (1.0 + r)
    comb_ref[rows, :] = jnp.where(lane == i1, w1, 0.0) + jnp.where(lane == i2, w2, 0.0)


def _xattn_router(h1, kvx, tokens_per_batch, norm_xattn, norm_moe, w_xq, w_xo, wr_hi, wr_lo, b_r, tm):
    t, d_model = h1.shape
    tm = min(tm, tokens_per_batch)
    per_b = tokens_per_batch // tm
    n_mem = kvx.shape[1]
    row = lambda width: pl.BlockSpec((tm, width), lambda i: (i, 0))
    return pl.pallas_call(
        functools.partial(_xattn_router_kernel, d_model=d_model, sub_rows=min(tm, XATTN_SUB_ROWS)),
        grid=(t // tm,),
        in_specs=[row(d_model),
                  pl.BlockSpec((1, n_mem, 2 * d_model), lambda i: (i // per_b, 0, 0)),
                  _const_spec((1, d_model)), _const_spec((1, d_model)),
                  _const_spec(w_xq.shape), _const_spec(w_xo.shape),
                  _const_spec(wr_hi.shape), _const_spec(wr_lo.shape), _const_spec(b_r.shape)],
        out_specs=[row(d_model), row(d_model), row(LANES)],
        out_shape=[jax.ShapeDtypeStruct((t, d_model), F32),
                   jax.ShapeDtypeStruct((t, d_model), BF16),
                   jax.ShapeDtypeStruct((t, LANES), F32)],
        compiler_params=_params(1),
        name="xattn_router",
    )(h1, kvx, norm_xattn.astype(F32).reshape(1, d_model), norm_moe.astype(F32).reshape(1, d_model),
      w_xq, w_xo, wr_hi, wr_lo, b_r)


def _moe_kernel(hm_ref, comb_ref, h2_ref, upper_ref, gmap_ref, wg_ref, wu_ref, wd_ref, gf_ref, o_ref,
                acc_ref, mem_ref, rank_ref, mem_t_ref, rank_t_ref, *, final_norm, cap):
    g = pl.program_id(1)
    tm = hm_ref.shape[0]
    n_local = wg_ref.shape[0]
    comb = comb_ref[...]
    lane = lax.broadcasted_iota(jnp.int32, (tm, LANES), 1)

    @pl.when(g == 0)
    def _():
        acc_ref[...] = jnp.zeros(acc_ref.shape, F32)
        chosen = (comb > 0.0).astype(BF16)
        member = (jnp.dot(chosen, gmap_ref[...], preferred_element_type=F32) > 0.0).astype(F32)
        member_t = member.T
        rank_t = jnp.dot(member_t.astype(BF16), upper_ref[...], preferred_element_type=F32)
        mem_ref[...] = member
        mem_t_ref[...] = member_t
        rank_t_ref[...] = rank_t
        rank_ref[...] = rank_t.T

    mem_row = mem_t_ref[pl.ds(g, 1), :]
    rank_row = rank_t_ref[pl.ds(g, 1), :]
    pick = lane == g
    mem_col = jnp.sum(jnp.where(pick, mem_ref[...], 0.0), axis=-1, keepdims=True)
    rank_col = jnp.sum(jnp.where(pick, rank_ref[...], 0.0), axis=-1, keepdims=True)
    count = jnp.sum(mem_row).astype(jnp.int32)
    comb_hi, comb_lo = _split_bf16(comb)
    hm = hm_ref[...]

    def chunk(c, carry):
        base = (c * cap).astype(F32)
        slot_r = lax.broadcasted_iota(jnp.int32, (cap, tm), 0).astype(F32) + base
        sel = ((rank_row == slot_r) & (mem_row > 0.0)).astype(BF16)
        xc = jnp.dot(sel, hm, preferred_element_type=F32).astype(BF16)
        wc = (jnp.dot(sel, comb_hi, preferred_element_type=F32)
              + jnp.dot(sel, comb_lo, preferred_element_type=F32))
        lane_c = lax.broadcasted_iota(jnp.int32, (cap, LANES), 1)
        yc = jnp.zeros((cap, hm.shape[1]), F32)
        for e in range(n_local):
            w = jnp.sum(jnp.where(lane_c == g * n_local + e, wc, 0.0), axis=-1, keepdims=True)
            hg = jnp.dot(xc, wg_ref[e], preferred_element_type=F32)
            hu = jnp.dot(xc, wu_ref[e], preferred_element_type=F32)
            act = (_silu(hg) * hu * w).astype(BF16)
            yc = yc + jnp.dot(act, wd_ref[e], preferred_element_type=F32)
        slot_c = lax.broadcasted_iota(jnp.int32, (tm, cap), 1).astype(F32) + base
        back = ((rank_col == slot_c) & (mem_col > 0.0)).astype(BF16)
        acc_ref[...] += jnp.dot(back, yc.astype(BF16), preferred_element_type=F32)
        return carry

    lax.fori_loop(0, (count + cap - 1) // cap, chunk, 0)

    @pl.when(g == pl.num_programs(1) - 1)
    def _():
        h3 = h2_ref[...] + acc_ref[...]
        if final_norm:
            h3 = h3 * lax.rsqrt(jnp.mean(h3 * h3, axis=-1, keepdims=True) + RMS_EPS) * gf_ref[...]
        o_ref[...] = h3


def _moe(hm, comb, h2, w_gate, w_up, w_down, norm_final, final_norm, tm, cap):
    t, d_model = hm.shape
    n_exp, _, d_exp = w_gate.shape
    n_local = EXPERTS_PER_GROUP
    n_groups = n_exp // n_local
    tm = min(tm, t)
    cap = min(cap, tm)
    idx = jnp.arange(tm)
    upper = (idx[:, None] < idx[None, :]).astype(BF16)
    lanes = jnp.arange(LANES)
    gmap = ((lanes[:, None] // n_local == lanes[None, :]) & (lanes[:, None] < n_exp)).astype(BF16)
    row = lambda width, **kw: pl.BlockSpec((tm, width), lambda i, g: (i, 0), **kw)
    once = dict(pipeline_mode=pl.Buffered(1))
    return pl.pallas_call(
        functools.partial(_moe_kernel, final_norm=final_norm, cap=cap),
        grid=(t // tm, n_groups),
        in_specs=[row(d_model), row(LANES), row(d_model, **once),
                  pl.BlockSpec((tm, tm), lambda i, g: (0, 0), **once),
                  pl.BlockSpec((LANES, LANES), lambda i, g: (0, 0), **once),
                  pl.BlockSpec((n_local, d_model, d_exp), lambda i, g: (g, 0, 0)),
                  pl.BlockSpec((n_local, d_model, d_exp), lambda i, g: (g, 0, 0)),
                  pl.BlockSpec((n_local, d_exp, d_model), lambda i, g: (g, 0, 0)),
                  pl.BlockSpec((1, d_model), lambda i, g: (0, 0))],
        out_specs=row(d_model, **once),
        out_shape=jax.ShapeDtypeStruct((t, d_model), F32),
        scratch_shapes=[pltpu.VMEM((tm, d_model), F32),
                        pltpu.VMEM((tm, LANES), F32), pltpu.VMEM((tm, LANES), F32),
                        pltpu.VMEM((LANES, tm), F32), pltpu.VMEM((LANES, tm), F32)],
        compiler_params=_params(2),
        name="moe_experts",
    )(hm, comb, h2, upper, gmap, w_gate, w_up, w_down, norm_final.astype(F32).reshape(1, d_model))


def kernel(x, mem, positions, norm_mix, w_in, conv_w, conv_b, dt_bias, a_log, d_skip, ssd_norm, w_ssd_out, q_a_norm, w_q_b, kv_a_norm, w_kv_b, w_mla_out, w_o, norm_xattn, norm_mem, w_xq, w_xkv, w_xo, norm_moe, w_router_group, b_router_group, w_router_expert, b_router_expert, w_exp_gate, w_exp_up, w_exp_down, norm_final):
    b, s, d_model = x.shape
    t = b * s
    n_mem = mem.shape[1]
    depth = norm_mix.shape[0]
    d_inner = ssd_norm.shape[1]
    xbc_dim = conv_w.shape[2]
    heads = a_log.shape[1]
    q_rank, kv_rank = w_q_b.shape[1], w_kv_b.shape[1]
    n_groups = w_router_group.shape[2]
    n_exp = w_router_expert.shape[2]

    o_z, o_xbc = 0, d_inner
    o_dt = o_xbc + xbc_dim
    o_qa = o_dt + heads
    o_gs = o_qa + q_rank + kv_rank + QK_ROPE
    o_end = o_gs + 2 * d_model

    h = x.reshape(t, d_model).astype(F32)
    for i in range(depth):
        wi = w_in[i]
        w_z = wi[:, o_z:o_xbc].astype(BF16)
        w_xbc = wi[:, o_xbc:o_dt].astype(BF16)
        qkv_width = q_rank + kv_rank + LANES
        w_small = jnp.concatenate([jnp.pad(wi[:, o_qa:o_gs], ((0, 0), (0, LANES - QK_ROPE))),
                                   jnp.pad(wi[:, o_dt:o_qa], ((0, 0), (0, LANES - heads)))], axis=1).astype(BF16)
        w_gates = wi[:, o_gs:o_end].astype(BF16)

        zb = _norm_matmul(h, norm_mix[i], w_z, BF16, 1024, d_inner).reshape(b, s, d_inner)
        xbc = _norm_matmul(h, norm_mix[i], w_xbc, BF16, 1024, xbc_dim).reshape(b, s, xbc_dim)
        gates = _norm_matmul(h, norm_mix[i], w_gates, BF16, 1024, 2 * d_model)
        small = _norm_matmul(h, norm_mix[i], w_small, F32, 1024, w_small.shape[1])

        yn = _ssd(xbc, zb, small.reshape(b, s, w_small.shape[1]), qkv_width // LANES, conv_w[i], conv_b[i],
                  dt_bias[i], a_log[i], d_skip[i], ssd_norm[i])

        hd = MLA_HEADS * LANES
        q, k, v = _mla_prep(small, qkv_width, positions, q_a_norm[i], w_q_b[i], kv_a_norm[i], w_kv_b[i], 512)
        o = _flash(q.reshape(b, s, hd), k.reshape(b, s, hd), v.reshape(b, s, hd), FLASH_TQ, FLASH_TK, FLASH_HP)

        h1 = _merge(yn.reshape(t, d_inner), o.reshape(t, MLA_HEADS * V_HEAD), gates, h, w_ssd_out[i].astype(BF16),
                    w_mla_out[i].astype(BF16), w_o[i].astype(BF16), 512)

        kvx = _norm_matmul(mem.reshape(b * n_mem, d_model).astype(F32), norm_mem[i], w_xkv[i].astype(BF16), BF16,
                           b * n_mem, 1024).reshape(b, n_mem, 2 * d_model)
        w_r = jnp.concatenate([jnp.pad(w_router_group[i].astype(F32), ((0, 0), (0, LANES - n_groups))),
                               jnp.pad(w_router_expert[i].astype(F32), ((0, 0), (0, LANES - n_exp)))], axis=1)
        wr_hi = w_r.astype(BF16)
        wr_lo = (w_r - wr_hi.astype(F32)).astype(BF16)
        b_r = jnp.concatenate([jnp.pad(b_router_group[i].astype(F32), (0, LANES - n_groups), constant_values=NEG),
                               jnp.pad(b_router_expert[i].astype(F32), (0, LANES - n_exp), constant_values=NEG)]
                              ).reshape(1, 2 * LANES)
        h2, hm, comb = _xattn_router(h1, kvx, s, norm_xattn[i], norm_moe[i], w_xq[i].astype(BF16),
                                     w_xo[i].astype(BF16), wr_hi, wr_lo, b_r, XATTN_TM)

        h = _moe(hm, comb, h2, w_exp_gate[i].astype(BF16), w_exp_up[i].astype(BF16), w_exp_down[i].astype(BF16),
                 norm_final, i == depth - 1, MOE_TM, MOE_CAP)
    return h.reshape(b, s, d_model)
```

```python
import functools

import jax
import jax.numpy as jnp
from jax import lax
from jax.experimental import pallas as pl
from jax.experimental.pallas import tpu as pltpu

F32 = jnp.float32
BF16 = jnp.bfloat16
RMS_EPS = 1e-6
NEG = -1e30
ROPE_THETA = 10000.0
LOG2_E = 1.4426950408889634

LANES = 128
SSD_CHUNK = 128
SSD_SUB = 2
SSD_HEAD_DIM = 64
SSD_GROUPS = 4
SSD_STATE = 128
CONV_WIDTH = 4
MLA_HEADS = 16
QK_NOPE = 64
QK_ROPE = 32
V_HEAD = 64
XA_HEADS = 4
EXPERTS_PER_GROUP = 8
VMEM_LIMIT = 56 * 1024 * 1024
FLASH_TQ = 1024
FLASH_TK = 512
FLASH_HP = 2
XATTN_TM = 1024
XATTN_SUB_ROWS = 512
MOE_TM = 1024
MOE_CAP = 320


def _params(n_axes):
    return pltpu.CompilerParams(dimension_semantics=("arbitrary",) * n_axes,
                                vmem_limit_bytes=VMEM_LIMIT)


def _sigmoid(v):
    return 1.0 / (1.0 + jnp.exp(-v))


def _silu(v):
    return v * _sigmoid(v)


def _split_bf16(v):
    hi = v.astype(BF16)
    lo = (v - hi.astype(F32)).astype(BF16)
    return hi, lo


def _const_spec(shape):
    return pl.BlockSpec(shape, lambda *_: (0,) * len(shape))


def _norm_matmul_kernel(x_ref, g_ref, w_ref, o_ref, un_ref):
    @pl.when(pl.program_id(1) == 0)
    def _():
        x = x_ref[...]
        r = lax.rsqrt(jnp.mean(x * x, axis=-1, keepdims=True) + RMS_EPS)
        un_ref[...] = (x * r * g_ref[...]).astype(BF16)

    o_ref[...] = jnp.dot(un_ref[...], w_ref[...], preferred_element_type=F32).astype(o_ref.dtype)


def _norm_matmul(x, gain, w, out_dtype, tm, tn):
    m, k = x.shape
    n = w.shape[1]
    tm, tn = min(tm, m), min(tn, n)
    return pl.pallas_call(
        _norm_matmul_kernel,
        grid=(m // tm, n // tn),
        in_specs=[pl.BlockSpec((tm, k), lambda i, j: (i, 0)),
                  pl.BlockSpec((1, k), lambda i, j: (0, 0)),
                  pl.BlockSpec((k, tn), lambda i, j: (0, j))],
        out_specs=pl.BlockSpec((tm, tn), lambda i, j: (i, j)),
        out_shape=jax.ShapeDtypeStruct((m, n), out_dtype),
        scratch_shapes=[pltpu.VMEM((tm, k), BF16)],
        compiler_params=_params(2),
        name="norm_matmul",
    )(x, gain.reshape(1, k), w)


def _ssd_kernel(xbc_ref, z_ref, dtr_ref, cw_ref, cb_ref, shift_ref, dtb_ref, alog_ref, expand_ref, dskip_ref,
                gn_ref, o_ref, cbuf_ref, state_ref, *, d_inner, heads):
    l = SSD_CHUNK
    n_sub = xbc_ref.shape[1] // l

    @pl.when(pl.program_id(1) == 0)
    def _():
        cbuf_ref[0:l, :] = jnp.zeros((l, cbuf_ref.shape[1]), cbuf_ref.dtype)
        state_ref[...] = jnp.zeros(state_ref.shape, F32)

    cbuf_ref[l:, :] = xbc_ref[0]
    for sub in range(n_sub):
        _ssd_chunk(sub, xbc_ref, z_ref, dtr_ref, cw_ref, cb_ref, shift_ref, dtb_ref, alog_ref, expand_ref,
                   dskip_ref, gn_ref, o_ref, cbuf_ref, state_ref, d_inner=d_inner, heads=heads)
    cbuf_ref[0:l, :] = xbc_ref[0, (n_sub - 1) * l:, :]


def _ssd_chunk(sub, xbc_ref, z_ref, dtr_ref, cw_ref, cb_ref, shift_ref, dtb_ref, alog_ref, expand_ref, dskip_ref,
               gn_ref, o_ref, cbuf_ref, state_ref, *, d_inner, heads):
    l = SSD_CHUNK
    gn = SSD_GROUPS * SSD_STATE
    gw = d_inner // SSD_GROUPS
    heads_per_group = heads // SSD_GROUPS
    rows = slice(sub * l, (sub + 1) * l)

    cur = xbc_ref[0, rows, :]
    shifted = jnp.dot(shift_ref[...], cbuf_ref[sub * l:(sub + 2) * l, :],
                      preferred_element_type=F32)
    conv = cb_ref[...] + cw_ref[CONV_WIDTH - 1:CONV_WIDTH, :] * cur.astype(F32)
    for lag in range(1, CONV_WIDTH):
        k = CONV_WIDTH - 1 - lag
        conv = conv + cw_ref[k:k + 1, :] * shifted[(lag - 1) * l:lag * l, :]
    conv = _silu(conv)
    xs = conv[:, :d_inner]
    bm = conv[:, d_inner:d_inner + gn]
    cm = conv[:, d_inner + gn:]

    dtv = dtr_ref[0, rows, :] + dtb_ref[...]
    dt = jnp.maximum(dtv, 0.0) + jnp.log1p(jnp.exp(-jnp.abs(dtv)))
    a_dt = dt * (-jnp.exp(alog_ref[...]))
    row_i = lax.broadcasted_iota(jnp.int32, (l, l), 0)
    col_i = lax.broadcasted_iota(jnp.int32, (l, l), 1)
    causal = col_i <= row_i
    tri = causal.astype(BF16)
    hi, lo = _split_bf16(a_dt)
    a_cum = jnp.dot(tri, hi, preferred_element_type=F32) + jnp.dot(tri, lo, preferred_element_type=F32)
    a_cum_t = a_cum.T

    ex = expand_ref[...]

    def expand(v):
        vh, vl = _split_bf16(v)
        return jnp.dot(vh, ex, preferred_element_type=F32) + jnp.dot(vl, ex, preferred_element_type=F32)

    dt_x = expand(dt)
    a_cum_x = expand(a_cum)
    last = a_cum_x[l - 1:l, :]
    grow = jnp.exp(a_cum_x)
    to_end = jnp.exp(last - a_cum_x)
    chunk_decay = jnp.exp(last)

    x_dt = xs * dt_x
    x_dt_b = x_dt.astype(BF16)
    x_end_b = (x_dt * to_end).astype(BF16)

    lane = lax.broadcasted_iota(jnp.int32, (l, LANES), 1)
    first_head = lane < SSD_HEAD_DIM
    outs = []
    for g in range(SSD_GROUPS):
        bg = bm[:, g * SSD_STATE:(g + 1) * SSD_STATE]
        cg = cm[:, g * SSD_STATE:(g + 1) * SSD_STATE].astype(BF16)
        cbm = lax.dot_general(cg, bg.astype(BF16), (((1,), (1,)), ((), ())), preferred_element_type=F32)
        cols = slice(g * gw, (g + 1) * gw)
        diag = []
        for pair in range(heads_per_group // 2):
            h0 = g * heads_per_group + 2 * pair
            xp = x_dt_b[:, h0 * SSD_HEAD_DIM:(h0 + 2) * SSD_HEAD_DIM]
            ys = []
            for h in (h0, h0 + 1):
                seg = a_cum[:, h:h + 1] - a_cum_t[h:h + 1, :]
                mh = (cbm * jnp.exp(jnp.where(causal, seg, NEG))).astype(BF16)
                ys.append(jnp.dot(mh, xp, preferred_element_type=F32))
            diag.append(jnp.where(first_head, ys[0], ys[1]))
        st = state_ref[g]
        y_off = jnp.dot(cg, st.astype(BF16), preferred_element_type=F32) * grow[:, cols]
        state_ref[g] = st * chunk_decay[:, cols] + jnp.dot(bg.T.astype(BF16), x_end_b[:, cols],
                                                          preferred_element_type=F32)
        outs.append(jnp.concatenate(diag, axis=1) + y_off)
    y = jnp.concatenate(outs, axis=1) + dskip_ref[...] * xs

    zz = z_ref[0, rows, :].astype(F32)
    y = y * _silu(zz)
    normed = []
    for g in range(SSD_GROUPS):
        yg = y[:, g * gw:(g + 1) * gw]
        normed.append(yg * lax.rsqrt(jnp.mean(yg * yg, axis=-1, keepdims=True) + RMS_EPS))
    o_ref[0, rows, :] = (jnp.concatenate(normed, axis=1) * gn_ref[...]).astype(o_ref.dtype)


def _ssd(xbc, z, dtr, dt_tile, conv_w, conv_b, dt_bias, a_log, d_skip, ssd_norm):
    b, s, xdim = xbc.shape
    d_inner = z.shape[2]
    heads = a_log.shape[0]
    l = SSD_CHUNK
    pad = LANES - heads
    dtb = jnp.pad(dt_bias.astype(F32), (0, pad)).reshape(1, LANES)
    alog = jnp.pad(a_log.astype(F32), (0, pad)).reshape(1, LANES)
    expand = (jnp.arange(LANES)[:, None] == (jnp.arange(d_inner)[None, :] // SSD_HEAD_DIM)).astype(BF16)
    dskip = jnp.repeat(d_skip.astype(F32), SSD_HEAD_DIM).reshape(1, d_inner)
    step = jnp.arange(l)
    shift = jnp.concatenate([(jnp.arange(2 * l)[None, :] == (l + step - lag)[:, None]).astype(BF16)
                             for lag in range(1, CONV_WIDTH)], axis=0)
    kern = functools.partial(_ssd_kernel, d_inner=d_inner, heads=heads)
    rows = SSD_SUB * l
    return pl.pallas_call(
        kern,
        grid=(b, s // rows),
        in_specs=[pl.BlockSpec((1, rows, xdim), lambda i, c: (i, c, 0)),
                  pl.BlockSpec((1, rows, d_inner), lambda i, c: (i, c, 0)),
                  pl.BlockSpec((1, rows, LANES), lambda i, c: (i, c, dt_tile)),
                  _const_spec((CONV_WIDTH, xdim)),
                  _const_spec((1, xdim)),
                  _const_spec(((CONV_WIDTH - 1) * l, 2 * l)),
                  _const_spec((1, LANES)),
                  _const_spec((1, LANES)),
                  _const_spec((LANES, d_inner)),
                  _const_spec((1, d_inner)),
                  _const_spec((1, d_inner))],
        out_specs=pl.BlockSpec((1, rows, d_inner), lambda i, c: (i, c, 0)),
        out_shape=jax.ShapeDtypeStruct((b, s, d_inner), BF16),
        scratch_shapes=[pltpu.VMEM((l + rows, xdim), BF16),
                        pltpu.VMEM((SSD_GROUPS, SSD_STATE, d_inner // SSD_GROUPS), F32)],
        compiler_params=_params(2),
        name="ssd_scan",
    )(xbc, z, dtr, conv_w.astype(F32), conv_b.astype(F32).reshape(1, xdim), shift, dtb, alog, expand, dskip,
      ssd_norm.astype(F32).reshape(1, d_inner))


def _mla_prep_kernel(a_ref, pos_ref, invf_ref, qg_ref, kg_ref, wq1_ref, wq2_ref, wk_ref, wv_ref,
                     q_ref, k_ref, v_ref, *, q_rank, kv_rank, scale):
    a = a_ref[...]
    qa = a[:, :q_rank]
    ckv = a[:, q_rank:q_rank + kv_rank]
    rest = a[:, q_rank + kv_rank:]
    qn = (qa * lax.rsqrt(jnp.mean(qa * qa, axis=-1, keepdims=True) + RMS_EPS) * qg_ref[...]).astype(BF16)
    cn = (ckv * lax.rsqrt(jnp.mean(ckv * ckv, axis=-1, keepdims=True) + RMS_EPS) * kg_ref[...]).astype(BF16)

    ang = pos_ref[...].astype(F32) * invf_ref[...]
    cos = jnp.cos(ang)
    sin = jnp.sin(ang)
    q = jnp.dot(qn, wq1_ref[...], preferred_element_type=F32)
    q_rot = jnp.dot(qn, wq2_ref[...], preferred_element_type=F32)

    lane = lax.broadcasted_iota(jnp.int32, ang.shape, 1)
    first_half = lane < QK_NOPE + QK_ROPE // 2
    kpe = pltpu.roll(rest, QK_NOPE, 1)
    kpe_rot = jnp.where(first_half, -pltpu.roll(kpe, LANES - QK_ROPE // 2, 1), pltpu.roll(kpe, QK_ROPE // 2, 1))
    kpe = kpe * cos + kpe_rot * sin
    k_nope = jnp.dot(cn, wk_ref[...], preferred_element_type=F32)
    for h in range(MLA_HEADS):
        cols = slice(h * LANES, (h + 1) * LANES)
        q_ref[:, cols] = ((q[:, cols] * cos + q_rot[:, cols] * sin) * scale).astype(q_ref.dtype)
        k_ref[:, cols] = (k_nope[:, cols] + kpe).astype(k_ref.dtype)

    v = jnp.dot(cn, wv_ref[...], preferred_element_type=F32)
    lane = lax.broadcasted_iota(jnp.int32, v.shape, 1) % LANES
    v_ref[...] = jnp.where(lane == V_HEAD, 1.0, v).astype(v_ref.dtype)


def _mla_prep(qkv_a, width, positions, q_a_norm, w_q_b, kv_a_norm, w_kv_b, tm):
    t = qkv_a.shape[0]
    q_rank, kv_rank = w_q_b.shape[0], w_kv_b.shape[0]
    hd = MLA_HEADS * LANES
    qk_head = QK_NOPE + QK_ROPE
    half = QK_ROPE // 2
    tm = min(tm, t)

    w3 = w_q_b.astype(F32).reshape(q_rank, MLA_HEADS, qk_head)
    wq1 = jnp.pad(w3, ((0, 0), (0, 0), (0, LANES - qk_head))).reshape(q_rank, hd).astype(BF16)
    pe = w3[..., QK_NOPE:]
    rot = jnp.concatenate([-pe[..., half:], pe[..., :half]], axis=-1)
    wq2 = jnp.pad(rot, ((0, 0), (0, 0), (QK_NOPE, LANES - qk_head))).reshape(q_rank, hd).astype(BF16)
    kv3 = w_kv_b.astype(F32).reshape(kv_rank, MLA_HEADS, QK_NOPE + V_HEAD)
    wk = jnp.pad(kv3[..., :QK_NOPE], ((0, 0), (0, 0), (0, LANES - QK_NOPE))).reshape(kv_rank, hd).astype(BF16)
    wv = jnp.pad(kv3[..., QK_NOPE:], ((0, 0), (0, 0), (0, LANES - V_HEAD))).reshape(kv_rank, hd).astype(BF16)

    inv_freq = ROPE_THETA ** (-jnp.arange(half, dtype=F32) / half)
    invf = jnp.concatenate([jnp.zeros((QK_NOPE,), F32), inv_freq, inv_freq,
                            jnp.zeros((LANES - qk_head,), F32)]).reshape(1, LANES)

    kern = functools.partial(_mla_prep_kernel, q_rank=q_rank, kv_rank=kv_rank,
                             scale=qk_head ** -0.5 * LOG2_E)
    out = jax.ShapeDtypeStruct((t, hd), BF16)
    return pl.pallas_call(
        kern,
        grid=(t // tm,),
        in_specs=[pl.BlockSpec((tm, width), lambda i: (i, 0)),
                  pl.BlockSpec((tm, 1), lambda i: (i, 0)),
                  _const_spec((1, LANES)),
                  _const_spec((1, q_rank)),
                  _const_spec((1, kv_rank)),
                  _const_spec((q_rank, hd)),
                  _const_spec((q_rank, hd)),
                  _const_spec((kv_rank, hd)),
                  _const_spec((kv_rank, hd))],
        out_specs=[pl.BlockSpec((tm, hd), lambda i: (i, 0))] * 3,
        out_shape=[out, out, out],
        compiler_params=_params(1),
        name="mla_prep",
    )(qkv_a, positions.reshape(t, 1), invf, q_a_norm.astype(F32).reshape(1, q_rank),
      kv_a_norm.astype(F32).reshape(1, kv_rank), wq1, wq2, wk, wv)


def _flash_kernel(q_ref, k_ref, v_ref, o_ref, s_ref, m_ref, acc_ref, *, tq, tk, hp):
    i = pl.program_id(2)
    diag_blocks = tq // tk
    assert diag_blocks == 2
    n_tiles = tk // LANES
    m_ref[...] = jnp.full(m_ref.shape, NEG, F32)
    acc_ref[...] = jnp.zeros(acc_ref.shape, F32)

    def rows(j):
        return pl.ds(pl.multiple_of(j * tk, tk), tk)

    def head(h):
        return slice(h * LANES, (h + 1) * LANES)

    def scores(j, slot, row0=0):
        for h in range(hp):
            s_ref[2 * h + slot, row0:, :] = lax.dot_general(
                q_ref[0, row0:, head(h)], k_ref[0, rows(j), head(h)],
                (((1,), (1,)), ((), ())), preferred_element_type=F32)

    def update(j, slot, key_offset=None, row0=0):
        nr = tq - row0
        for h in range(hp):
            tiles = [s_ref[2 * h + slot, row0:, c * LANES:(c + 1) * LANES] for c in range(n_tiles)]
            if key_offset is not None:
                r = lax.broadcasted_iota(jnp.int32, (nr, LANES), 0) + row0
                c0 = lax.broadcasted_iota(jnp.int32, (nr, LANES), 1)
                tiles = [jnp.where(c0 + (key_offset + c * LANES) <= r, t, NEG) for c, t in enumerate(tiles)]
            m_old = m_ref[h, row0:, :]
            m_new = jnp.maximum(m_old, jnp.max(functools.reduce(jnp.maximum, tiles), axis=-1, keepdims=True))
            p = jnp.concatenate([jnp.exp2(t - m_new) for t in tiles], axis=1).astype(BF16)
            acc_ref[h, row0:, :] = (jnp.exp2(m_old - m_new) * acc_ref[h, row0:, :]
                                    + jnp.dot(p, v_ref[0, rows(j), head(h)], preferred_element_type=F32))
            m_ref[h, row0:, :] = m_new

    scores(0, 0)

    def two_blocks(j):
        scores(j + 1, 1)
        update(j, 0)
        scores(j + 2, 0)
        update(j + 1, 1)

    def body(jj, carry):
        two_blocks(4 * jj)
        two_blocks(4 * jj + 2)
        return carry

    lax.fori_loop(0, i // 2, body, 0)

    @pl.when(i % 2 == 1)
    def _():
        two_blocks(2 * i - 2)

    scores(2 * i + 1, 1, row0=tk)
    update(2 * i, 0, key_offset=0)
    update(2 * i + 1, 1, key_offset=tk, row0=tk)
    outs = []
    for h in range(hp):
        acc = acc_ref[h]
        outs.append(acc[:, :V_HEAD] / acc[:, V_HEAD:V_HEAD + 1])
    o_ref[0] = jnp.concatenate(outs, axis=1).astype(o_ref.dtype)


def _flash(q, k, v, tq, tk, hp):
    b, s, hd = q.shape
    w = hp * LANES
    return pl.pallas_call(
        functools.partial(_flash_kernel, tq=tq, tk=tk, hp=hp),
        grid=(b, hd // w, s // tq),
        in_specs=[pl.BlockSpec((1, tq, w), lambda bi, h, i: (bi, i, h)),
                  pl.BlockSpec((1, s, w), lambda bi, h, i: (bi, 0, h)),
                  pl.BlockSpec((1, s, w), lambda bi, h, i: (bi, 0, h))],
        out_specs=pl.BlockSpec((1, tq, hp * V_HEAD), lambda bi, h, i: (bi, i, h)),
        out_shape=jax.ShapeDtypeStruct((b, s, (hd // LANES) * V_HEAD), BF16),
        scratch_shapes=[pltpu.VMEM((2 * hp, tq, tk), F32),
                        pltpu.VMEM((hp, tq, LANES), F32), pltpu.VMEM((hp, tq, LANES), F32)],
        compiler_params=_params(3),
        name="mla_flash",
    )(q, k, v)


def _merge_kernel(yn_ref, o_ref, g_ref, x_ref, wso_ref, wmo_ref, wo_ref, h_ref, *, d_model):
    y_ssd = jnp.dot(yn_ref[...], wso_ref[...], preferred_element_type=F32)
    y_mla = jnp.dot(o_ref[...], wmo_ref[...], preferred_element_type=F32)
    g = g_ref[...].astype(F32)
    merged = _sigmoid(g[:, :d_model]) * y_ssd + _sigmoid(g[:, d_model:]) * y_mla
    h_ref[...] = x_ref[...] + jnp.dot(merged.astype(BF16), wo_ref[...], preferred_element_type=F32)


def _merge(yn, o, gates, x, w_ssd_out, w_mla_out, w_o, tm):
    t, d_model = x.shape
    tm = min(tm, t)
    row = lambda width: pl.BlockSpec((tm, width), lambda i: (i, 0))
    return pl.pallas_call(
        functools.partial(_merge_kernel, d_model=d_model),
        grid=(t // tm,),
        in_specs=[row(yn.shape[1]), row(o.shape[1]), row(gates.shape[1]), row(d_model),
                  _const_spec(w_ssd_out.shape), _const_spec(w_mla_out.shape), _const_spec(w_o.shape)],
        out_specs=row(d_model),
        out_shape=jax.ShapeDtypeStruct((t, d_model), F32),
        compiler_params=_params(1),
        name="merge",
    )(yn, o, gates, x, w_ssd_out, w_mla_out, w_o)


def _xattn_router_kernel(h_ref, kv_ref, gx_ref, gm_ref, wq_ref, wo_ref, wrh_ref, wrl_ref, br_ref,
                         h2_ref, hm_ref, comb_ref, *, d_model, sub_rows):
    for start in range(0, h_ref.shape[0], sub_rows):
        rows = slice(start, start + sub_rows)
        _xattn_router_rows(rows, h_ref, kv_ref, gx_ref, gm_ref, wq_ref, wo_ref, wrh_ref, wrl_ref, br_ref,
                           h2_ref, hm_ref, comb_ref, d_model=d_model)


def _xattn_router_rows(rows, h_ref, kv_ref, gx_ref, gm_ref, wq_ref, wo_ref, wrh_ref, wrl_ref, br_ref,
                       h2_ref, hm_ref, comb_ref, *, d_model):
    h1 = h_ref[rows, :]
    hn = (h1 * lax.rsqrt(jnp.mean(h1 * h1, axis=-1, keepdims=True) + RMS_EPS) * gx_ref[...]).astype(BF16)
    q = jnp.dot(hn, wq_ref[...], preferred_element_type=F32).astype(BF16)
    hd = d_model // XA_HEADS
    kv = kv_ref[0]
    outs = []
    for hh in range(XA_HEADS):
        qh = q[:, hh * hd:(hh + 1) * hd]
        kh = kv[:, hh * hd:(hh + 1) * hd]
        vh = kv[:, d_model + hh * hd:d_model + (hh + 1) * hd]
        s = lax.dot_general(qh, kh, (((1,), (1,)), ((), ())), preferred_element_type=F32) * (hd ** -0.5)
        p = jnp.exp(s - jnp.max(s, axis=-1, keepdims=True))
        p = p / jnp.sum(p, axis=-1, keepdims=True)
        outs.append(jnp.dot(p.astype(BF16), vh, preferred_element_type=F32))
    ox = jnp.concatenate(outs, axis=1).astype(BF16)
    h2 = h1 + jnp.dot(ox, wo_ref[...], preferred_element_type=F32)
    h2_ref[rows, :] = h2

    hm = h2 * lax.rsqrt(jnp.mean(h2 * h2, axis=-1, keepdims=True) + RMS_EPS) * gm_ref[...]
    hm_ref[rows, :] = hm.astype(hm_ref.dtype)

    hi, lo = _split_bf16(hm)
    logits = (jnp.dot(hi, wrh_ref[...], preferred_element_type=F32)
              + jnp.dot(lo, wrh_ref[...], preferred_element_type=F32)
              + jnp.dot(hi, wrl_ref[...], preferred_element_type=F32)) + br_ref[...]
    gl = logits[:, :LANES]
    el = logits[:, LANES:]
    lane = lax.broadcasted_iota(jnp.int32, gl.shape, 1)

    def first_argmax(v, vmax):
        return jnp.min(jnp.where(v == vmax, lane, LANES), axis=-1, keepdims=True)

    gmax = jnp.max(gl, axis=-1, keepdims=True)
    g_sel = first_argmax(gl, gmax)
    g_w = 1.0 / jnp.sum(jnp.exp(gl - gmax), axis=-1, keepdims=True)
    lo_lane = g_sel * EXPERTS_PER_GROUP
    in_group = (lane >= lo_lane) & (lane < lo_lane + EXPERTS_PER_GROUP)
    e1 = jnp.where(in_group, el, NEG)
    v1 = jnp.max(e1, axis=-1, keepdims=True)
    i1 = first_argmax(e1, v1)
    e2 = jnp.where(lane == i1, NEG, e1)
    v2 = jnp.max(e2, axis=-1, keepdims=True)
    i2 = first_argmax(e2, v2)
    r = jnp.exp(v2 - v1)
    w1 = g_w / (1.0 + r)
    w2 = g_w * r / (1.0 + r)
    comb_ref[rows, :] = jnp.where(lane == i1, w1, 0.0) + jnp.where(lane == i2, w2, 0.0)


def _xattn_router(h1, kvx, tokens_per_batch, norm_xattn, norm_moe, w_xq, w_xo, wr_hi, wr_lo, b_r, tm):
    t, d_model = h1.shape
    tm = min(tm, tokens_per_batch)
    per_b = tokens_per_batch // tm
    n_mem = kvx.shape[1]
    row = lambda width: pl.BlockSpec((tm, width), lambda i: (i, 0))
    return pl.pallas_call(
        functools.partial(_xattn_router_kernel, d_model=d_model, sub_rows=min(tm, XATTN_SUB_ROWS)),
        grid=(t // tm,),
        in_specs=[row(d_model),
                  pl.BlockSpec((1, n_mem, 2 * d_model), lambda i: (i // per_b, 0, 0)),
                  _const_spec((1, d_model)), _const_spec((1, d_model)),
                  _const_spec(w_xq.shape), _const_spec(w_xo.shape),
                  _const_spec(wr_hi.shape), _const_spec(wr_lo.shape), _const_spec(b_r.shape)],
        out_specs=[row(d_model), row(d_model), row(LANES)],
        out_shape=[jax.ShapeDtypeStruct((t, d_model), F32),
                   jax.ShapeDtypeStruct((t, d_model), BF16),
                   jax.ShapeDtypeStruct((t, LANES), F32)],
        compiler_params=_params(1),
        name="xattn_router",
    )(h1, kvx, norm_xattn.astype(F32).reshape(1, d_model), norm_moe.astype(F32).reshape(1, d_model),
      w_xq, w_xo, wr_hi, wr_lo, b_r)


def _moe_kernel(hm_ref, comb_ref, h2_ref, upper_ref, gmap_ref, wg_ref, wu_ref, wd_ref, gf_ref, o_ref,
                acc_ref, mem_ref, rank_ref, mem_t_ref, rank_t_ref, *, final_norm, cap):
    g = pl.program_id(1)
    tm = hm_ref.shape[0]
    n_local = wg_ref.shape[0]
    comb = comb_ref[...]
    lane = lax.broadcasted_iota(jnp.int32, (tm, LANES), 1)

    @pl.when(g == 0)
    def _():
        acc_ref[...] = jnp.zeros(acc_ref.shape, F32)
        chosen = (comb > 0.0).astype(BF16)
        member = (jnp.dot(chosen, gmap_ref[...], preferred_element_type=F32) > 0.0).astype(F32)
        member_t = member.T
        rank_t = jnp.dot(member_t.astype(BF16), upper_ref[...], preferred_element_type=F32)
        mem_ref[...] = member
        mem_t_ref[...] = member_t
        rank_t_ref[...] = rank_t
        rank_ref[...] = rank_t.T

    mem_row = mem_t_ref[pl.ds(g, 1), :]
    rank_row = rank_t_ref[pl.ds(g, 1), :]
    pick = lane == g
    mem_col = jnp.sum(jnp.where(pick, mem_ref[...], 0.0), axis=-1, keepdims=True)
    rank_col = jnp.sum(jnp.where(pick, rank_ref[...], 0.0), axis=-1, keepdims=True)
    count = jnp.sum(mem_row).astype(jnp.int32)
    comb_hi, comb_lo = _split_bf16(comb)
    hm = hm_ref[...]

    def chunk(c, carry):
        base = (c * cap).astype(F32)
        slot_r = lax.broadcasted_iota(jnp.int32, (cap, tm), 0).astype(F32) + base
        sel = ((rank_row == slot_r) & (mem_row > 0.0)).astype(BF16)
        xc = jnp.dot(sel, hm, preferred_element_type=F32).astype(BF16)
        wc = (jnp.dot(sel, comb_hi, preferred_element_type=F32)
              + jnp.dot(sel, comb_lo, preferred_element_type=F32))
        lane_c = lax.broadcasted_iota(jnp.int32, (cap, LANES), 1)
        yc = jnp.zeros((cap, hm.shape[1]), F32)
        for e in range(n_local):
            w = jnp.sum(jnp.where(lane_c == g * n_local + e, wc, 0.0), axis=-1, keepdims=True)
            hg = jnp.dot(xc, wg_ref[e], preferred_element_type=F32)
            hu = jnp.dot(xc, wu_ref[e], preferred_element_type=F32)
            act = (_silu(hg) * hu * w).astype(BF16)
            yc = yc + jnp.dot(act, wd_ref[e], preferred_element_type=F32)
        slot_c = lax.broadcasted_iota(jnp.int32, (tm, cap), 1).astype(F32) + base
        back = ((rank_col == slot_c) & (mem_col > 0.0)).astype(BF16)
        acc_ref[...] += jnp.dot(back, yc.astype(BF16), preferred_element_type=F32)
        return carry

    lax.fori_loop(0, (count + cap - 1) // cap, chunk, 0)

    @pl.when(g == pl.num_programs(1) - 1)
    def _():
        h3 = h2_ref[...] + acc_ref[...]
        if final_norm:
            h3 = h3 * lax.rsqrt(jnp.mean(h3 * h3, axis=-1, keepdims=True) + RMS_EPS) * gf_ref[...]
        o_ref[...] = h3


def _moe(hm, comb, h2, w_gate, w_up, w_down, norm_final, final_norm, tm, cap):
    t, d_model = hm.shape
    n_exp, _, d_exp = w_gate.shape
    n_local = EXPERTS_PER_GROUP
    n_groups = n_exp // n_local
    tm = min(tm, t)
    cap = min(cap, tm)
    idx = jnp.arange(tm)
    upper = (idx[:, None] < idx[None, :]).astype(BF16)
    lanes = jnp.arange(LANES)
    gmap = ((lanes[:, None] // n_local == lanes[None, :]) & (lanes[:, None] < n_exp)).astype(BF16)
    row = lambda width, **kw: pl.BlockSpec((tm, width), lambda i, g: (i, 0), **kw)
    once = dict(pipeline_mode=pl.Buffered(1))
    return pl.pallas_call(
        functools.partial(_moe_kernel, final_norm=final_norm, cap=cap),
        grid=(t // tm, n_groups),
        in_specs=[row(d_model), row(LANES), row(d_model, **once),
                  pl.BlockSpec((tm, tm), lambda i, g: (0, 0), **once),
                  pl.BlockSpec((LANES, LANES), lambda i, g: (0, 0), **once),
                  pl.BlockSpec((n_local, d_model, d_exp), lambda i, g: (g, 0, 0)),
                  pl.BlockSpec((n_local, d_model, d_exp), lambda i, g: (g, 0, 0)),
                  pl.BlockSpec((n_local, d_exp, d_model), lambda i, g: (g, 0, 0)),
                  pl.BlockSpec((1, d_model), lambda i, g: (0, 0))],
        out_specs=row(d_model, **once),
        out_shape=jax.ShapeDtypeStruct((t, d_model), F32),
        scratch_shapes=[pltpu.VMEM((tm, d_model), F32),
                        pltpu.VMEM((tm, LANES), F32), pltpu.VMEM((tm, LANES), F32),
                        pltpu.VMEM((LANES, tm), F32), pltpu.VMEM((LANES, tm), F32)],
        compiler_params=_params(2),
        name="moe_experts",
    )(hm, comb, h2, upper, gmap, w_gate, w_up, w_down, norm_final.astype(F32).reshape(1, d_model))


def kernel(x, mem, positions, norm_mix, w_in, conv_w, conv_b, dt_bias, a_log, d_skip, ssd_norm, w_ssd_out, q_a_norm, w_q_b, kv_a_norm, w_kv_b, w_mla_out, w_o, norm_xattn, norm_mem, w_xq, w_xkv, w_xo, norm_moe, w_router_group, b_router_group, w_router_expert, b_router_expert, w_exp_gate, w_exp_up, w_exp_down, norm_final):
    b, s, d_model = x.shape
    t = b * s
    n_mem = mem.shape[1]
    depth = norm_mix.shape[0]
    d_inner = ssd_norm.shape[1]
    xbc_dim = conv_w.shape[2]
    heads = a_log.shape[1]
    q_rank, kv_rank = w_q_b.shape[1], w_kv_b.shape[1]
    n_groups = w_router_group.shape[2]
    n_exp = w_router_expert.shape[2]

    o_z, o_xbc = 0, d_inner
    o_dt = o_xbc + xbc_dim
    o_qa = o_dt + heads
    o_gs = o_qa + q_rank + kv_rank + QK_ROPE
    o_end = o_gs + 2 * d_model

    h = x.reshape(t, d_model).astype(F32)
    for i in range(depth):
        wi = w_in[i]
        w_z = wi[:, o_z:o_xbc].astype(BF16)
        w_xbc = wi[:, o_xbc:o_dt].astype(BF16)
        qkv_width = q_rank + kv_rank + LANES
        w_small = jnp.concatenate([jnp.pad(wi[:, o_qa:o_gs], ((0, 0), (0, LANES - QK_ROPE))),
                                   jnp.pad(wi[:, o_dt:o_qa], ((0, 0), (0, LANES - heads)))], axis=1).astype(BF16)
        w_gates = wi[:, o_gs:o_end].astype(BF16)

        zb = _norm_matmul(h, norm_mix[i], w_z, BF16, 1024, d_inner).reshape(b, s, d_inner)
        xbc = _norm_matmul(h, norm_mix[i], w_xbc, BF16, 1024, xbc_dim).reshape(b, s, xbc_dim)
        gates = _norm_matmul(h, norm_mix[i], w_gates, BF16, 1024, 2 * d_model)
        small = _norm_matmul(h, norm_mix[i], w_small, F32, 1024, w_small.shape[1])

        yn = _ssd(xbc, zb, small.reshape(b, s, w_small.shape[1]), qkv_width // LANES, conv_w[i], conv_b[i],
                  dt_bias[i], a_log[i], d_skip[i], ssd_norm[i])

        hd = MLA_HEADS * LANES
        q, k, v = _mla_prep(small, qkv_width, positions, q_a_norm[i], w_q_b[i], kv_a_norm[i], w_kv_b[i], 512)
        o = _flash(q.reshape(b, s, hd), k.reshape(b, s, hd), v.reshape(b, s, hd), FLASH_TQ, FLASH_TK, FLASH_HP)

        h1 = _merge(yn.reshape(t, d_inner), o.reshape(t, MLA_HEADS * V_HEAD), gates, h, w_ssd_out[i].astype(BF16),
                    w_mla_out[i].astype(BF16), w_o[i].astype(BF16), 512)

        kvx = _norm_matmul(mem.reshape(b * n_mem, d_model).astype(F32), norm_mem[i], w_xkv[i].astype(BF16), BF16,
                           b * n_mem, 1024).reshape(b, n_mem, 2 * d_model)
        w_r = jnp.concatenate([jnp.pad(w_router_group[i].astype(F32), ((0, 0), (0, LANES - n_groups))),
                               jnp.pad(w_router_expert[i].astype(F32), ((0, 0), (0, LANES - n_exp)))], axis=1)
        wr_hi = w_r.astype(BF16)
        wr_lo = (w_r - wr_hi.astype(F32)).astype(BF16)
        b_r = jnp.concatenate([jnp.pad(b_router_group[i].astype(F32), (0, LANES - n_groups), constant_values=NEG),
                               jnp.pad(b_router_expert[i].astype(F32), (0, LANES - n_exp), constant_values=NEG)]
                              ).reshape(1, 2 * LANES)
        h2, hm, comb = _xattn_router(h1, kvx, s, norm_xattn[i], norm_moe[i], w_xq[i].astype(BF16),
                                     w_xo[i].astype(BF16), wr_hi, wr_lo, b_r, XATTN_TM)

        h = _moe(hm, comb, h2, w_exp_gate[i].astype(BF16), w_exp_up[i].astype(BF16), w_exp_down[i].astype(BF16),
                 norm_final, i == depth - 1, MOE_TM, MOE_CAP)
    return h.reshape(b, s, d_model)
```
